```python
import math
import jax
import jax.numpy as jnp
from jax import lax
import numpy as np

D_MODEL = 2048
BATCH = 4
SEQ = 2048
DEPTH = 4
DEC_BATCH = 128
DEC_SEQ = 1
PAST_LEN = 16384
PAGE_SIZE = 128

GDN_QK_HEADS = 16
GDN_V_HEADS = 32
GDN_HEAD_DIM = 128
GDN_QK_W = GDN_QK_HEADS * GDN_HEAD_DIM
GDN_V_W = GDN_V_HEADS * GDN_HEAD_DIM
GDN_QKV_W = 2 * GDN_QK_W + GDN_V_W
GDN_CONV_W = 4
GDN_CHUNK = 64
S5_GROUP_CH = 16
S5_STATE = 64
S5_W = D_MODEL
S5_GROUPS = S5_W // S5_GROUP_CH
N_EXPERTS = 64
TOP_K = 8
D_EXPERT = D_MODEL // 4
D_SHARED = D_EXPERT
ROUTED_SCALE = 2.5
MOE_BLOCK = 128
IN_WIDTHS = (GDN_QK_W, GDN_QK_W, GDN_V_W, GDN_V_W, GDN_V_HEADS, GDN_V_HEADS, S5_W, D_MODEL, D_MODEL)
IN_W = sum(IN_WIDTHS)
DEEPNORM_ALPHA = (2 * DEPTH) ** 0.25
DEEPNORM_BETA = (8 * DEPTH) ** -0.25
NORM_EPS = 1e-6

kernel_name = 'hybrid_gdn_s5_moe_decoder_step'


def layer_norm(x, g, b):
    xf = x.astype(jnp.float32)
    mu = jnp.mean(xf, axis=-1, keepdims=True)
    var = jnp.mean(jnp.square(xf - mu), axis=-1, keepdims=True)
    y = (xf - mu) * lax.rsqrt(var + NORM_EPS) * g.astype(jnp.float32) + b.astype(jnp.float32)
    return y.astype(x.dtype)


def rms_norm(x, g):
    xf = x.astype(jnp.float32)
    return xf * lax.rsqrt(jnp.mean(jnp.square(xf), axis=-1, keepdims=True) + NORM_EPS) * g.astype(jnp.float32)


def l2_normalize(x):
    xf = x.astype(jnp.float32)
    return xf * lax.rsqrt(jnp.sum(jnp.square(xf), axis=-1, keepdims=True) + NORM_EPS)


def split_cols(x, widths):
    out, start = [], 0
    for w in widths:
        out.append(x[..., start:start + w])
        start += w
    return out


def causal_conv(x, buf, w):
    seq = x.shape[1]
    xp = jnp.concatenate([buf.astype(x.dtype), x], axis=1)
    y = xp[:, 0:seq] * w[0]
    for j in range(1, GDN_CONV_W):
        y = y + xp[:, j:j + seq] * w[j]
    return y, xp[:, seq:]


def gated_delta_rule(q, k, v, g, beta, s0):
    bsz, seq = q.shape[:2]
    n_chunks = -(-seq // GDN_CHUNK)
    pad = n_chunks * GDN_CHUNK - seq

    def chunked(t):
        t = jnp.pad(t, [(0, 0), (0, pad)] + [(0, 0)] * (t.ndim - 2))
        t = t.reshape((bsz, n_chunks, GDN_CHUNK) + t.shape[2:])
        return jnp.moveaxis(t, (1, 2), (0, 3))

    qc, kc, vc, gc, bc = chunked(q), chunked(k), chunked(v), chunked(g), chunked(beta)
    G = jnp.cumsum(gc, axis=-1)
    idx = jnp.arange(GDN_CHUNK)
    causal = idx[:, None] >= idx[None, :]
    strict = idx[:, None] > idx[None, :]
    decay = jnp.exp(jnp.where(causal, G[..., :, None] - G[..., None, :], -jnp.inf))
    kb = kc * bc[..., None]
    a_mat = jnp.where(strict, jnp.einsum('nbhik,nbhjk->nbhij', kb, kc) * decay, 0.0)
    rhs = jnp.concatenate([vc * bc[..., None], kb * jnp.exp(G)[..., None]], axis=-1)
    sol = lax.linalg.triangular_solve(a_mat, rhs, left_side=True, lower=True, unit_diagonal=True)
    u_c, w_c = sol[..., :GDN_HEAD_DIM], sol[..., GDN_HEAD_DIM:]
    qk = jnp.einsum('nbhik,nbhjk->nbhij', qc, kc) * decay
    q_dec = qc * jnp.exp(G)[..., None]
    k_dec = kc * jnp.exp(G[..., -1:] - G)[..., None]
    g_last = jnp.exp(G[..., -1])

    def step(S, xs):
        qd, kd, uu, ww, qkc, gl = xs
        v_new = uu - jnp.einsum('bhck,bhkv->bhcv', ww, S)
        o = jnp.einsum('bhck,bhkv->bhcv', qd, S) + jnp.einsum('bhij,bhjv->bhiv', qkc, v_new)
        S = S * gl[..., None, None] + jnp.einsum('bhck,bhcv->bhkv', kd, v_new)
        return S, o

    s_final, o = lax.scan(step, s0, (q_dec, k_dec, u_c, w_c, qk, g_last))
    o = jnp.moveaxis(o, (0, 3), (1, 2)).reshape(bsz, n_chunks * GDN_CHUNK, GDN_V_HEADS, GDN_HEAD_DIM)
    return o[:, :seq], s_final


def _ssm_combine(e1, e2):
    a1, b1 = e1
    a2, b2 = e2
    return a1 * a2, a2 * b1 + b2


def s5_scan(u, s0_re, s0_im, lam_re, lam_im, log_dt, b_re, b_im, c_re, c_im, d_skip):
    bsz, seq, _ = u.shape
    f32 = jnp.float32
    uf = u.astype(f32).reshape(bsz, seq, S5_GROUPS, S5_GROUP_CH)
    lam = lax.complex(lam_re.astype(f32), lam_im.astype(f32))
    dt = jnp.exp(log_dt.astype(f32))[:, None]
    lam_bar = jnp.exp(lam * dt)
    b_bar = ((lam_bar - 1.0) / lam)[..., None] * lax.complex(b_re.astype(f32), b_im.astype(f32))
    bu = jnp.einsum('gnp,btgp->btgn', b_bar, uf)
    x0 = lax.complex(s0_re.astype(f32), s0_im.astype(f32))
    bu = bu.at[:, 0].add(lam_bar * x0)
    a = jnp.broadcast_to(lam_bar, (1, seq) + lam_bar.shape)
    _, xs = lax.associative_scan(_ssm_combine, (a, bu), axis=1)
    c = lax.complex(c_re.astype(f32), c_im.astype(f32))
    y = jnp.einsum('gpn,btgn->btgp', c, xs).real + d_skip.astype(f32) * uf
    x_last = xs[:, -1]
    return y.reshape(bsz, seq, S5_W), x_last.real, x_last.imag


def mixer_block(h, conv_s, gdn_s, re_s, im_s, w_in, conv_w, a_log, dt_bias, norm_w,
                lam_re, lam_im, log_dt, b_re, b_im, c_re, c_im, d_skip, w_glu, w_br_a, w_br_b, w_out):
    bsz, seq, _ = h.shape
    f32 = jnp.float32
    proj = h @ w_in
    q, k, v, z, b_raw, a_raw, u, gate_a, gate_b = split_cols(proj, IN_WIDTHS)
    qkv, conv_n = causal_conv(jnp.concatenate([q, k, v], axis=-1), conv_s, conv_w)
    q, k, v = split_cols(jax.nn.silu(qkv), (GDN_QK_W, GDN_QK_W, GDN_V_W))
    rep = GDN_V_HEADS // GDN_QK_HEADS
    q = jnp.repeat(l2_normalize(q.reshape(bsz, seq, GDN_QK_HEADS, GDN_HEAD_DIM)) * GDN_HEAD_DIM ** -0.5, rep, axis=2)
    k = jnp.repeat(l2_normalize(k.reshape(bsz, seq, GDN_QK_HEADS, GDN_HEAD_DIM)), rep, axis=2)
    v = v.reshape(bsz, seq, GDN_V_HEADS, GDN_HEAD_DIM).astype(f32)
    beta = jax.nn.sigmoid(b_raw.astype(f32))
    g = -jnp.exp(a_log.astype(f32)) * jax.nn.softplus(a_raw.astype(f32) + dt_bias.astype(f32))
    o, gdn_n = gated_delta_rule(q, k, v, g, beta, gdn_s.astype(f32))
    o = rms_norm(o, norm_w) * jax.nn.silu(z.reshape(bsz, seq, GDN_V_HEADS, GDN_HEAD_DIM).astype(f32))
    y_a = o.reshape(bsz, seq, GDN_V_W).astype(h.dtype) @ w_br_a
    y_s, re_n, im_n = s5_scan(u, re_s, im_s, lam_re, lam_im, log_dt, b_re, b_im, c_re, c_im, d_skip)
    y_s = jax.nn.gelu(y_s).astype(h.dtype)
    y_s = y_s * jax.nn.sigmoid(y_s @ w_glu)
    y_b = y_s @ w_br_b
    merged = jax.nn.sigmoid(gate_a) * y_a + jax.nn.sigmoid(gate_b) * y_b
    return merged @ w_out, conv_n, gdn_n, re_n, im_n


def swiglu(x, wg, wu, wd):
    return (jax.nn.silu(x @ wg) * (x @ wu)) @ wd


def routed_experts(h, top_idx, top_w, w_gate, w_up, w_down):
    n_tok, d = h.shape
    n_assign = n_tok * TOP_K
    n_blocks = -(-n_assign // MOE_BLOCK) + N_EXPERTS
    n_rows = n_blocks * MOE_BLOCK
    flat_e = top_idx.reshape(-1)
    order = jnp.argsort(flat_e)
    sorted_e = flat_e[order]
    sizes = jnp.bincount(flat_e, length=N_EXPERTS)
    starts = jnp.cumsum(sizes) - sizes
    padded = (sizes + MOE_BLOCK - 1) // MOE_BLOCK * MOE_BLOCK
    pad_ends = jnp.cumsum(padded)
    pad_starts = pad_ends - padded
    dest = pad_starts[sorted_e] + jnp.arange(n_assign) - starts[sorted_e]
    row_tok = jnp.full((n_rows,), n_tok, jnp.int32).at[dest].set((order // TOP_K).astype(jnp.int32))
    row_w = jnp.zeros((n_rows,), jnp.float32).at[dest].set(top_w.reshape(-1)[order])
    block_e = jnp.minimum(jnp.searchsorted(pad_ends, jnp.arange(n_blocks) * MOE_BLOCK, side='right'), N_EXPERTS - 1)
    h_pad = jnp.concatenate([h, jnp.zeros((1, d), h.dtype)], axis=0)

    def one_block(args):
        tok, e = args
        xb = h_pad[tok]
        return swiglu(xb, w_gate[e], w_up[e], w_down[e])

    y = lax.map(one_block, (row_tok.reshape(n_blocks, MOE_BLOCK), block_e))
    y = y.reshape(n_rows, d) * row_w[:, None].astype(y.dtype)
    return jax.ops.segment_sum(y, row_tok, num_segments=n_tok + 1)[:n_tok]


def moe_block(h, w_router, router_bias, w_eg, w_eu, w_ed, w_sg, w_su, w_sd):
    bsz, seq, d = h.shape
    ht = h.reshape(bsz * seq, d)
    scores = jax.nn.sigmoid((ht @ w_router).astype(jnp.float32))
    _, top_idx = lax.top_k(scores + router_bias.astype(jnp.float32), TOP_K)
    sel = jnp.take_along_axis(scores, top_idx, axis=-1)
    top_w = sel / jnp.sum(sel, axis=-1, keepdims=True) * ROUTED_SCALE
    out = routed_experts(ht, top_idx, top_w, w_eg, w_eu, w_ed) + swiglu(ht, w_sg, w_su, w_sd)
    return out.reshape(bsz, seq, d)


def setup_inputs(seed: int = 0) -> dict:
    key = jax.random.key(seed)
    ks = iter(jax.random.split(key, 48))

    def nrm(shape, scale=1.0):
        return jax.random.normal(next(ks), shape, jnp.float32) * scale

    def unif(shape, lo, hi):
        return jax.random.uniform(next(ks), shape, jnp.float32, lo, hi)

    L, D = DEPTH, D_MODEL
    log_dt_lo, log_dt_hi = math.log(1e-3), math.log(1e-1)
    gdn_dt = jnp.exp(unif((L, GDN_V_HEADS), log_dt_lo, log_dt_hi))
    return {
        'x_prompt': nrm((BATCH, SEQ, D)),
        'x_sample': nrm((DEC_BATCH, DEC_SEQ, D)),
        'cache_conv': nrm((L, DEC_BATCH, GDN_CONV_W - 1, GDN_QKV_W)),
        'state_gdn': nrm((L, DEC_BATCH, GDN_V_HEADS, GDN_HEAD_DIM, GDN_HEAD_DIM), 0.5),
        'state_s5_re': nrm((L, DEC_BATCH, S5_GROUPS, S5_STATE), 0.5),
        'state_s5_im': nrm((L, DEC_BATCH, S5_GROUPS, S5_STATE), 0.5),
        'c_prompt': nrm((BATCH, D)),
        'c_sample': nrm((DEC_BATCH, D)),
        'ln_in_g': 1.0 + nrm((D,), 0.02),
        'ln_in_b': nrm((D,), 0.02),
        'w_ada': nrm((L, D, 6 * D), D ** -0.5),
        'b_ada': nrm((L, 6 * D), 0.01),
        'w_in': nrm((L, D, IN_W), D ** -0.5),
        'conv_w': nrm((L, GDN_CONV_W, GDN_QKV_W), GDN_CONV_W ** -0.5),
        'gdn_a_log': jnp.log(unif((L, GDN_V_HEADS), 1.0, 16.0)),
        'gdn_dt_bias': gdn_dt + jnp.log(-jnp.expm1(-gdn_dt)),
        'gdn_norm_w': 1.0 + nrm((L, GDN_HEAD_DIM), 0.02),
        's5_lam_re': -0.5 + nrm((L, S5_GROUPS, S5_STATE), 0.01),
        's5_lam_im': math.pi * jnp.arange(S5_STATE, dtype=jnp.float32) + nrm((L, S5_GROUPS, S5_STATE), 0.01),
        's5_log_dt': unif((L, S5_GROUPS), log_dt_lo, log_dt_hi),
        's5_b_re': nrm((L, S5_GROUPS, S5_STATE, S5_GROUP_CH), (2 * S5_GROUP_CH) ** -0.5),
        's5_b_im': nrm((L, S5_GROUPS, S5_STATE, S5_GROUP_CH), (2 * S5_GROUP_CH) ** -0.5),
        's5_c_re': nrm((L, S5_GROUPS, S5_GROUP_CH, S5_STATE), S5_STATE ** -0.5),
        's5_c_im': nrm((L, S5_GROUPS, S5_GROUP_CH, S5_STATE), S5_STATE ** -0.5),
        's5_d': nrm((L, S5_GROUPS, S5_GROUP_CH)),
        'w_glu': nrm((L, S5_W, S5_W), S5_W ** -0.5),
        'w_br_a': nrm((L, GDN_V_W, D), GDN_V_W ** -0.5),
        'w_br_b': nrm((L, S5_W, D), S5_W ** -0.5),
        'w_out': nrm((L, D, D), D ** -0.5 * DEEPNORM_BETA),
        'ln1_g': 1.0 + nrm((L, D), 0.02),
        'ln1_b': nrm((L, D), 0.02),
        'w_router': nrm((L, D, N_EXPERTS), D ** -0.5),
        'router_bias': nrm((L, N_EXPERTS), 0.01),
        'w_exp_gate': nrm((L, N_EXPERTS, D, D_EXPERT), D ** -0.5),
        'w_exp_up': nrm((L, N_EXPERTS, D, D_EXPERT), D ** -0.5),
        'w_exp_down': nrm((L, N_EXPERTS, D_EXPERT, D), D_EXPERT ** -0.5 * DEEPNORM_BETA),
        'w_sh_gate': nrm((L, D, D_SHARED), D ** -0.5),
        'w_sh_up': nrm((L, D, D_SHARED), D ** -0.5),
        'w_sh_down': nrm((L, D_SHARED, D), D_SHARED ** -0.5 * DEEPNORM_BETA),
        'ln2_g': 1.0 + nrm((L, D), 0.02),
        'ln2_b': nrm((L, D), 0.02),
    }


def reference(x_prompt, x_sample, cache_conv, state_gdn, state_s5_re, state_s5_im, c_prompt, c_sample,
              ln_in_g, ln_in_b, w_ada, b_ada, w_in, conv_w, gdn_a_log, gdn_dt_bias, gdn_norm_w,
              s5_lam_re, s5_lam_im, s5_log_dt, s5_b_re, s5_b_im, s5_c_re, s5_c_im, s5_d, w_glu,
              w_br_a, w_br_b, w_out, ln1_g, ln1_b, w_router, router_bias, w_exp_gate, w_exp_up,
              w_exp_down, w_sh_gate, w_sh_up, w_sh_down, ln2_g, ln2_b):

    def layer(x, c, conv_s, gdn_s, re_s, im_s, l):
        mod = (jax.nn.silu(c) @ w_ada[l] + b_ada[l])[:, None, :]
        sh1, sc1, g1, sh2, sc2, g2 = jnp.split(mod, 6, axis=-1)
        h = x * (1.0 + sc1) + sh1
        mix, conv_n, gdn_n, re_n, im_n = mixer_block(
            h, conv_s, gdn_s, re_s, im_s, w_in[l], conv_w[l], gdn_a_log[l], gdn_dt_bias[l], gdn_norm_w[l],
            s5_lam_re[l], s5_lam_im[l], s5_log_dt[l], s5_b_re[l], s5_b_im[l], s5_c_re[l], s5_c_im[l], s5_d[l],
            w_glu[l], w_br_a[l], w_br_b[l], w_out[l])
        x = layer_norm(DEEPNORM_ALPHA * x + g1 * mix, ln1_g[l], ln1_b[l])
        h = x * (1.0 + sc2) + sh2
        ffn = moe_block(h, w_router[l], router_bias[l], w_exp_gate[l], w_exp_up[l], w_exp_down[l],
                        w_sh_gate[l], w_sh_up[l], w_sh_down[l])
        x = layer_norm(DEEPNORM_ALPHA * x + g2 * ffn, ln2_g[l], ln2_b[l])
        return x, conv_n, gdn_n, re_n, im_n

    bp = x_prompt.shape[0]
    zero_conv = jnp.zeros((bp, GDN_CONV_W - 1, GDN_QKV_W), x_prompt.dtype)
    zero_gdn = jnp.zeros((bp, GDN_V_HEADS, GDN_HEAD_DIM, GDN_HEAD_DIM), jnp.float32)
    zero_s5 = jnp.zeros((bp, S5_GROUPS, S5_STATE), jnp.float32)

    x_p = layer_norm(x_prompt, ln_in_g, ln_in_b)
    x_s = layer_norm(x_sample, ln_in_g, ln_in_b)
    conv_p, gdn_p, re_p, im_p = [], [], [], []
    conv_q, gdn_q, re_q, im_q = [], [], [], []
    for l in range(DEPTH):
        x_p, cv, gd, sr, si = layer(x_p, c_prompt, zero_conv, zero_gdn, zero_s5, zero_s5, l)
        conv_p.append(cv); gdn_p.append(gd); re_p.append(sr); im_p.append(si)
        x_s, cv, gd, sr, si = layer(x_s, c_sample, cache_conv[l], state_gdn[l], state_s5_re[l], state_s5_im[l], l)
        conv_q.append(cv); gdn_q.append(gd); re_q.append(sr); im_q.append(si)

    return (x_p, x_s,
            jnp.stack(conv_p), jnp.stack(gdn_p), jnp.stack(re_p), jnp.stack(im_p),
            jnp.stack(conv_q), jnp.stack(gdn_q), jnp.stack(re_q), jnp.stack(im_q))
```

```python
import functools
import math

import jax
import jax.numpy as jnp
from jax import lax
from jax.experimental import pallas as pl
from jax.experimental.pallas import tpu as pltpu

F32 = jnp.float32
BF16 = jnp.bfloat16

D_MODEL = 2048
BATCH = 4
SEQ = 2048
DEPTH = 4
DEC_BATCH = 128
N_PROMPT = BATCH * SEQ
M_ROWS = N_PROMPT + DEC_BATCH

QK_HEADS = 16
V_HEADS = 32
HEAD_DIM = 128
QK_W = QK_HEADS * HEAD_DIM
V_W = V_HEADS * HEAD_DIM
QKV_W = 2 * QK_W + V_W
CONV_W = 4
CHUNK = 64
N_CHUNKS = SEQ // CHUNK

S5_P = 16
S5_N = 64
S5_G = D_MODEL // S5_P
S5_STATES = S5_G * S5_N

N_EXPERTS = 64
TOP_K = 8
D_EXPERT = D_MODEL // 4
ROUTED_SCALE = 2.5
N_ASSIGN = M_ROWS * TOP_K

ALPHA = (2 * DEPTH) ** 0.25
EPS = 1e-6

COL_Z = QKV_W
COL_BA = QKV_W + V_W
COL_TAIL = COL_BA + 2 * V_HEADS

LANES = 128
SUBLANES = 8
VMEM_LIMIT = 56 * 1024 * 1024

ROW_TILE = 128
N_ROW_TILES = M_ROWS // ROW_TILE
PROMPT_ROW_TILES = N_PROMPT // ROW_TILE
TILES_PER_SEQ = SEQ // ROW_TILE
MM_TM = 640

GDN_PAIRS = 4
GDN_GROUPS = QK_HEADS // GDN_PAIRS
GDN_VH = 2 * GDN_PAIRS

S5_TT = 32
S5_BLK = 8
S5_NBLK = S5_G // S5_BLK
S5_BS = S5_BLK * S5_N

MOE_BS = 256
MOE_BLOCKS = -(-N_ASSIGN // MOE_BS) + N_EXPERTS
MOE_ROWS = MOE_BLOCKS * MOE_BS
COMB_T = 128


def _cparams(sem, vmem=VMEM_LIMIT):
    return pltpu.CompilerParams(dimension_semantics=sem, vmem_limit_bytes=vmem)


def _sigmoid(x):
    return jax.nn.sigmoid(x)


def _silu(x):
    return x * jax.nn.sigmoid(x)


def _gelu_tanh(x):
    c = math.sqrt(2.0 / math.pi)
    return 0.5 * x * (1.0 + jnp.tanh(c * (x + 0.044715 * (x * x * x))))


def _layer_norm(x, g, b):
    mu = jnp.mean(x, axis=-1, keepdims=True)
    xc = x - mu
    var = jnp.mean(xc * xc, axis=-1, keepdims=True)
    return xc * lax.rsqrt(var + EPS) * g + b


def _mm_body(*refs, prologue, epilogue, n_extra):
    a_ref, w_ref = refs[0], refs[1]
    extra = refs[2:2 + n_extra]
    o_ref = refs[2 + n_extra]
    wbf_ref = refs[3 + n_extra]

    @pl.when(pl.program_id(1) == 0)
    def _():
        wbf_ref[...] = w_ref[...].astype(BF16)

    a = a_ref[...]
    if prologue == "silu":
        a = _silu(a.astype(F32))
    acc = jnp.dot(a.astype(BF16), wbf_ref[...], preferred_element_type=F32)
    if epilogue == "bias":
        acc = acc + extra[0][...]
    elif epilogue == "glu":
        acc = extra[0][...].astype(F32) * _sigmoid(acc)
    elif epilogue == "merge":
        ya, ga, gb = extra[0][...], extra[1][...], extra[2][...]
        acc = _sigmoid(ga) * ya + _sigmoid(gb) * acc
    o_ref[...] = acc.astype(o_ref.dtype)


def _mm(a, w, *, layer, col0, n, tn, tm=MM_TM, out_dtype=F32, prologue=None, epilogue=None,
        extra=(), extra_specs=(), rows=None, name="mm"):
    rows = a.shape[0] if rows is None else rows
    k = a.shape[1]
    assert rows % tm == 0 and n % tn == 0 and col0 % tn == 0
    cb0 = col0 // tn
    grid = (n // tn, rows // tm)
    in_specs = [
        pl.BlockSpec((tm, k), lambda j, i: (i, 0)),
        pl.BlockSpec((None, k, tn), lambda j, i: (layer, 0, cb0 + j)),
    ] + list(extra_specs)
    return pl.pallas_call(
        functools.partial(_mm_body, prologue=prologue, epilogue=epilogue, n_extra=len(extra)),
        grid=grid,
        in_specs=in_specs,
        out_specs=pl.BlockSpec((tm, tn), lambda j, i: (i, j)),
        out_shape=jax.ShapeDtypeStruct((rows, n), out_dtype),
        scratch_shapes=[pltpu.VMEM((k, tn), BF16)],
        compiler_params=_cparams(("arbitrary", "arbitrary")),
        name=name,
    )(a, w, *extra)


def _ada_body(c_ref, w_ref, b_ref, o_ref):
    a = _silu(c_ref[...]).astype(BF16)
    o_ref[...] = jnp.dot(a, w_ref[...].astype(BF16), preferred_element_type=F32) + b_ref[...]


def _ada_all(c_all, w_ada, b_ada):
    rows = c_all.shape[0]
    tn = 1024
    nj = (6 * D_MODEL) // tn
    return pl.pallas_call(
        _ada_body,
        grid=(DEPTH * nj,),
        in_specs=[
            pl.BlockSpec((rows, D_MODEL), lambda g: (0, 0)),
            pl.BlockSpec((None, D_MODEL, tn), lambda g: (g // nj, 0, g % nj)),
            pl.BlockSpec((None, 1, tn), lambda g: (g // nj, 0, g % nj)),
        ],
        out_specs=pl.BlockSpec((None, rows, tn), lambda g: (g // nj, 0, g % nj)),
        out_shape=jax.ShapeDtypeStruct((DEPTH, rows, 6 * D_MODEL), F32),
        compiler_params=_cparams(("arbitrary",)),
        name="ada",
    )(c_all, w_ada, b_ada.reshape(DEPTH, 1, 6 * D_MODEL))


def _mod_pick(i, mp_ref, ms_ref):
    b = jnp.minimum(i // TILES_PER_SEQ, BATCH - 1)
    row = mp_ref[pl.ds(b, 1), :]
    return jnp.where(i < PROMPT_ROW_TILES, row, ms_ref[...])


def _mod_specs(layer, comp):
    return [
        pl.BlockSpec((None, SUBLANES, D_MODEL), lambda i: (layer, 0, comp)),
        pl.BlockSpec((None, DEC_BATCH, D_MODEL), lambda i: (layer, 0, comp)),
    ]


def _row_spec():
    return pl.BlockSpec((ROW_TILE, D_MODEL), lambda i: (i, 0))


def _vec_spec():
    return pl.BlockSpec((1, D_MODEL), lambda i: (0, 0))


def _ln_in_body(x_ref, g_ref, b_ref, scp, scs, shp, shs, x_out, h_out):
    i = pl.program_id(0)
    x = _layer_norm(x_ref[...], g_ref[...], b_ref[...])
    x_out[...] = x
    h = x * (1.0 + _mod_pick(i, scp, scs)) + _mod_pick(i, shp, shs)
    h_out[...] = h.astype(h_out.dtype)


def _ln_in(x_all, g, b, mod_p, mod_s):
    return pl.pallas_call(
        _ln_in_body,
        grid=(N_ROW_TILES,),
        in_specs=[_row_spec(), _vec_spec(), _vec_spec()] + _mod_specs(0, 1) + _mod_specs(0, 0),
        out_specs=[_row_spec(), _row_spec()],
        out_shape=[jax.ShapeDtypeStruct((M_ROWS, D_MODEL), F32),
                   jax.ShapeDtypeStruct((M_ROWS, D_MODEL), BF16)],
        compiler_params=_cparams(("arbitrary",)),
        name="ln_in",
    )(x_all, g.reshape(1, -1), b.reshape(1, -1), mod_p, mod_s, mod_p, mod_s)


def _deepnorm_body(x_ref, y_ref, g_ref, b_ref, gtp, gts, scp, scs, shp, shs, x_out, h_out):
    i = pl.program_id(0)
    r = ALPHA * x_ref[...] + _mod_pick(i, gtp, gts) * y_ref[...]
    x = _layer_norm(r, g_ref[...], b_ref[...])
    x_out[...] = x
    h = x * (1.0 + _mod_pick(i, scp, scs)) + _mod_pick(i, shp, shs)
    h_out[...] = h.astype(h_out.dtype)


def _deepnorm(x, y, g, b, mod_p, mod_s, *, layer, gate_comp, mod_layer, sc_comp, sh_comp, h_dtype):
    return pl.pallas_call(
        _deepnorm_body,
        grid=(N_ROW_TILES,),
        in_specs=[_row_spec(), _row_spec(),
                  pl.BlockSpec((None, 1, D_MODEL), lambda i: (layer, 0, 0)),
                  pl.BlockSpec((None, 1, D_MODEL), lambda i: (layer, 0, 0))]
        + _mod_specs(layer, gate_comp) + _mod_specs(mod_layer, sc_comp) + _mod_specs(mod_layer, sh_comp),
        out_specs=[_row_spec(), _row_spec()],
        out_shape=[jax.ShapeDtypeStruct((M_ROWS, D_MODEL), F32),
                   jax.ShapeDtypeStruct((M_ROWS, D_MODEL), h_dtype)],
        compiler_params=_cparams(("arbitrary",)),
        name="deepnorm",
    )(x, y, g.reshape(DEPTH, 1, D_MODEL), b.reshape(DEPTH, 1, D_MODEL),
      mod_p, mod_s, mod_p, mod_s, mod_p, mod_s)


def _gates_body(ba_ref, alog_ref, dtb_ref, gb_ref, gc_ref):
    x = ba_ref[...]
    lane = lax.broadcasted_iota(jnp.int32, x.shape, 1)
    beta = _sigmoid(x)
    z = x + dtb_ref[...]
    softplus = jnp.maximum(z, 0.0) + jnp.log1p(jnp.exp(-jnp.abs(z)))
    g = -jnp.exp(alog_ref[...]) * softplus
    is_beta = lane < V_HEADS
    is_g = jnp.logical_and(lane >= V_HEADS, lane < 2 * V_HEADS)
    g = jnp.where(is_g, g, 0.0)
    gb_ref[...] = jnp.where(is_beta, beta, g)
    r = lax.broadcasted_iota(jnp.int32, (ROW_TILE, ROW_TILE), 0)
    c = lax.broadcasted_iota(jnp.int32, (ROW_TILE, ROW_TILE), 1)
    tri = jnp.logical_and(c <= r, (r // CHUNK) == (c // CHUNK)).astype(F32)
    gc_ref[...] = jnp.dot(tri, g, preferred_element_type=F32, precision=lax.Precision.HIGHEST)


def _gates(ba, alog_row, dtb_row):
    spec = pl.BlockSpec((ROW_TILE, LANES), lambda i: (i, 0))
    vec = pl.BlockSpec((1, LANES), lambda i: (0, 0))
    return pl.pallas_call(
        _gates_body,
        grid=(N_ROW_TILES,),
        in_specs=[spec, vec, vec],
        out_specs=[spec, spec],
        out_shape=[jax.ShapeDtypeStruct((M_ROWS, LANES), F32)] * 2,
        compiler_params=_cparams(("arbitrary",)),
        name="gates",
    )(ba, alog_row, dtb_row)


def _inv_unit_lower(a, row, col):
    hp = lax.Precision.HIGHEST
    dot = lambda x, y: jnp.dot(x, y, preferred_element_type=F32, precision=hp)
    eye = (row == col).astype(F32)
    same8 = (row >> 3) == (col >> 3)
    d = jnp.where(same8, a, 0.0)
    d2 = dot(d, d)
    d4 = dot(d2, d2)
    t = eye - d
    t = t + dot(t, d2)
    t = t + dot(t, d4)
    for sh in (4, 5, 6):
        off = jnp.logical_and((row >> sh) == (col >> sh), (row >> (sh - 1)) != (col >> (sh - 1)))
        e = jnp.where(off, a, 0.0)
        t = t - dot(t, dot(e, t))
    return t


def _conv_silu(x_ref, hist_ref, w_ref):
    x = x_ref[...]
    w = w_ref[...]
    xc = jnp.concatenate([hist_ref[...], x], axis=0)
    base = SUBLANES - (CONV_W - 1)
    y = xc[base:base + CHUNK] * w[0:1]
    for j in range(1, CONV_W - 1):
        y = y + xc[base + j:base + j + CHUNK] * w[j:j + 1]
    y = y + x * w[CONV_W - 1:CONV_W]
    hist_ref[...] = x[CHUNK - SUBLANES:CHUNK]
    return _silu(y)


def _gdn_prompt_body(q_ref, k_ref, v_ref, wq_ref, wk_ref, wv_ref, z_ref, gcol_ref, grow_ref, bcol_ref,
                     nw_ref, o_ref, s_ref, hq_ref, hk_ref, hv_ref):
    c = pl.program_id(2)

    @pl.when(c == 0)
    def _():
        s_ref[...] = jnp.zeros_like(s_ref)
        hq_ref[...] = jnp.zeros_like(hq_ref)
        hk_ref[...] = jnp.zeros_like(hk_ref)
        hv_ref[...] = jnp.zeros_like(hv_ref)

    qa = _conv_silu(q_ref, hq_ref, wq_ref)
    ka = _conv_silu(k_ref, hk_ref, wk_ref)
    va = _conv_silu(v_ref, hv_ref, wv_ref)
    z = z_ref[...]
    gcol = gcol_ref[...]
    grow = grow_ref[...]
    bcol = bcol_ref[...]
    nw = nw_ref[...]

    row = lax.broadcasted_iota(jnp.int32, (CHUNK, CHUNK), 0)
    col = lax.broadcasted_iota(jnp.int32, (CHUNK, CHUNK), 1)
    causal = row >= col
    strict = row > col
    dotb = lambda x, y: jnp.dot(x.astype(BF16), y.astype(BF16), preferred_element_type=F32)

    for p in range(GDN_PAIRS):
        q = qa[:, p * HEAD_DIM:(p + 1) * HEAD_DIM]
        k = ka[:, p * HEAD_DIM:(p + 1) * HEAD_DIM]
        qn = q * lax.rsqrt(jnp.sum(q * q, axis=-1, keepdims=True) + EPS) * (HEAD_DIM ** -0.5)
        kn = k * lax.rsqrt(jnp.sum(k * k, axis=-1, keepdims=True) + EPS)
        knt = kn.T
        kk = dotb(kn, knt)
        qk = dotb(qn, knt)
        for hh in range(2):
            j = 2 * p + hh
            gc = gcol[:, j:j + 1]
            gr = grow[j:j + 1, :]
            bc = bcol[:, j:j + 1]
            dec = jnp.exp(jnp.where(causal, gc - gr, -jnp.inf))
            a = jnp.where(strict, bc * kk * dec, 0.0)
            t = _inv_unit_lower(a, row, col)
            eg = jnp.exp(gc)
            v = va[:, j * HEAD_DIM:(j + 1) * HEAD_DIM]
            rhs = jnp.concatenate([v * bc, kn * (bc * eg)], axis=1)
            sol = dotb(t, rhs)
            u = sol[:, :HEAD_DIM]
            w = sol[:, HEAD_DIM:]
            s = s_ref[j]
            v_new = u - dotb(w, s)
            o = dotb(qn * eg, s) + dotb(qk * dec, v_new)
            gl = gc[CHUNK - 1:CHUNK, :]
            s_ref[j] = s * jnp.exp(gl) + dotb(knt, v_new * jnp.exp(gl - gc))
            zz = z[:, j * HEAD_DIM:(j + 1) * HEAD_DIM]
            on = o * lax.rsqrt(jnp.mean(o * o, axis=-1, keepdims=True) + EPS) * nw * _silu(zz)
            o_ref[:, j * HEAD_DIM:(j + 1) * HEAD_DIM] = on.astype(o_ref.dtype)


def _gdn_prompt(qkv, z, conv_w, gcol, grow, bcol, norm_w, *, layer):
    qw = GDN_PAIRS * HEAD_DIM
    vw = GDN_VH * HEAD_DIM
    kb0 = QK_W // qw
    vb0 = 2 * QK_W // vw
    rowblk = lambda b, h, c: b * N_CHUNKS + c
    gspec = lambda shape: pl.BlockSpec((None, None, None) + shape, lambda b, h, c: (b, c, h, 0, 0))
    return pl.pallas_call(
        _gdn_prompt_body,
        grid=(BATCH, GDN_GROUPS, N_CHUNKS),
        in_specs=[
            pl.BlockSpec((CHUNK, qw), lambda b, h, c: (rowblk(b, h, c), h)),
            pl.BlockSpec((CHUNK, qw), lambda b, h, c: (rowblk(b, h, c), kb0 + h)),
            pl.BlockSpec((CHUNK, vw), lambda b, h, c: (rowblk(b, h, c), vb0 + h)),
            pl.BlockSpec((None, CONV_W, qw), lambda b, h, c: (layer, 0, h)),
            pl.BlockSpec((None, CONV_W, qw), lambda b, h, c: (layer, 0, kb0 + h)),
            pl.BlockSpec((None, CONV_W, vw), lambda b, h, c: (layer, 0, vb0 + h)),
            pl.BlockSpec((CHUNK, vw), lambda b, h, c: (rowblk(b, h, c), h)),
            gspec((CHUNK, GDN_VH)),
            gspec((GDN_VH, CHUNK)),
            gspec((CHUNK, GDN_VH)),
            pl.BlockSpec((None, 1, HEAD_DIM), lambda b, h, c: (layer, 0, 0)),
        ],
        out_specs=[
            pl.BlockSpec((CHUNK, vw), lambda b, h, c: (rowblk(b, h, c), h)),
            pl.BlockSpec((None, GDN_VH, HEAD_DIM, HEAD_DIM), lambda b, h, c: (b, h, 0, 0)),
        ],
        out_shape=[jax.ShapeDtypeStruct((N_PROMPT, V_W), BF16),
                   jax.ShapeDtypeStruct((BATCH, V_HEADS, HEAD_DIM, HEAD_DIM), F32)],
        scratch_shapes=[pltpu.VMEM((SUBLANES, qw), F32), pltpu.VMEM((SUBLANES, qw), F32),
                        pltpu.VMEM((SUBLANES, vw), F32)],
        compiler_params=_cparams(("arbitrary", "arbitrary", "arbitrary")),
        name="gdn_prompt",
    )(qkv, qkv, qkv, conv_w, conv_w, conv_w, z, gcol, grow, bcol, norm_w.reshape(DEPTH, 1, HEAD_DIM))


def _gdn_sample_prep_body(x_ref, c_ref, w_ref, o_ref, *, normalize):
    w = w_ref[...]
    y = c_ref[0] * w[0:1]
    for j in range(1, CONV_W - 1):
        y = y + c_ref[j] * w[j:j + 1]
    y = y + x_ref[...] * w[CONV_W - 1:CONV_W]
    y = _silu(y)
    if normalize:
        is_q = pl.program_id(0) < QK_W // y.shape[1]
        scale = jnp.where(is_q, HEAD_DIM ** -0.5, 1.0)
        for h in range(y.shape[1] // HEAD_DIM):
            yh = y[:, h * HEAD_DIM:(h + 1) * HEAD_DIM]
            yh = yh * lax.rsqrt(jnp.sum(yh * yh, axis=-1, keepdims=True) + EPS) * scale
            o_ref[:, h * HEAD_DIM:(h + 1) * HEAD_DIM] = yh
    else:
        o_ref[...] = y


def _gdn_sample_prep(qkv, cache_t, conv_w, *, layer, col0, n, normalize):
    tn = 1024
    cb0 = col0 // tn
    rb = N_PROMPT // DEC_BATCH
    return pl.pallas_call(
        functools.partial(_gdn_sample_prep_body, normalize=normalize),
        grid=(n // tn,),
        in_specs=[
            pl.BlockSpec((DEC_BATCH, tn), lambda j: (rb, cb0 + j)),
            pl.BlockSpec((None, CONV_W - 1, DEC_BATCH, tn), lambda j: (layer, 0, 0, cb0 + j)),
            pl.BlockSpec((None, CONV_W, tn), lambda j: (layer, 0, cb0 + j)),
        ],
        out_specs=pl.BlockSpec((DEC_BATCH, tn), lambda j: (0, j)),
        out_shape=jax.ShapeDtypeStruct((DEC_BATCH, n), F32),
        compiler_params=_cparams(("arbitrary",)),
        name="gdn_sample_prep",
    )(qkv, cache_t, conv_w)


def _gdn_sample_body(s_ref, qt_ref, kt_ref, v_ref, z_ref, eg_ref, beta_ref, nw_ref, so_ref, o_ref):
    qt = qt_ref[...]
    kt = kt_ref[...]
    v = v_ref[...]
    z = z_ref[...]
    egs = eg_ref[...]
    betas = beta_ref[...]
    nw = nw_ref[...]
    for h in range(V_HEADS):
        hq = h // 2
        kc = kt[:, hq:hq + 1]
        qc = qt[:, hq:hq + 1]
        s = s_ref[h]
        eg = egs[:, h:h + 1]
        beta = betas[:, h:h + 1]
        ks = jnp.sum(s * kc, axis=0, keepdims=True)
        qs = jnp.sum(s * qc, axis=0, keepdims=True)
        vh = v[:, h * HEAD_DIM:(h + 1) * HEAD_DIM]
        v_new = beta * vh - (beta * eg) * ks
        qk = jnp.sum(qc * kc, axis=0, keepdims=True)
        o = eg * qs + qk * v_new
        so_ref[h] = s * eg + kc * v_new
        zz = z[:, h * HEAD_DIM:(h + 1) * HEAD_DIM]
        on = o * lax.rsqrt(jnp.mean(o * o, axis=-1, keepdims=True) + EPS) * nw * _silu(zz)
        o_ref[:, h * HEAD_DIM:(h + 1) * HEAD_DIM] = on.astype(o_ref.dtype)


def _gdn_sample(state, qt, kt, v, z, eg, beta, norm_w, *, layer):
    r3 = lambda w: pl.BlockSpec((None, 1, w), lambda b: (b, 0, 0))
    return pl.pallas_call(
        _gdn_sample_body,
        grid=(DEC_BATCH,),
        in_specs=[
            pl.BlockSpec((None, None, V_HEADS, HEAD_DIM, HEAD_DIM), lambda b: (layer, b, 0, 0, 0)),
            pl.BlockSpec((None, HEAD_DIM, QK_HEADS), lambda b: (b, 0, 0)),
            pl.BlockSpec((None, HEAD_DIM, QK_HEADS), lambda b: (b, 0, 0)),
            r3(V_W), r3(V_W), r3(V_HEADS), r3(V_HEADS),
            pl.BlockSpec((None, 1, HEAD_DIM), lambda b: (layer, 0, 0)),
        ],
        out_specs=[
            pl.BlockSpec((None, V_HEADS, HEAD_DIM, HEAD_DIM), lambda b: (b, 0, 0, 0)),
            r3(V_W),
        ],
        out_shape=[jax.ShapeDtypeStruct((DEC_BATCH, V_HEADS, HEAD_DIM, HEAD_DIM), F32),
                   jax.ShapeDtypeStruct((DEC_BATCH, 1, V_W), BF16)],
        compiler_params=_cparams(("arbitrary",)),
        name="gdn_sample",
    )(state, qt, kt, v, z, eg, beta, norm_w.reshape(DEPTH, 1, HEAD_DIM))


def _s5_in_proj(ub, wb_ref, re_ref, im_ref):
    for j in range(S5_NBLK):
        r = jnp.dot(ub[:, j * LANES:(j + 1) * LANES], wb_ref[j], preferred_element_type=F32)
        re_ref[:, j * S5_BS:(j + 1) * S5_BS] = r[:, :S5_BS]
        im_ref[:, j * S5_BS:(j + 1) * S5_BS] = r[:, S5_BS:]


def _s5_out_proj(u, re_ref, im_ref, wc_ref, d_ref, y_ref):
    for j in range(S5_NBLK):
        xr = re_ref[:, j * S5_BS:(j + 1) * S5_BS].astype(BF16)
        xi = im_ref[:, j * S5_BS:(j + 1) * S5_BS].astype(BF16)
        y = jnp.dot(xr, wc_ref[j, :S5_BS], preferred_element_type=F32)
        y = y + jnp.dot(xi, wc_ref[j, S5_BS:], preferred_element_type=F32)
        y = y + d_ref[:, j * LANES:(j + 1) * LANES] * u[:, j * LANES:(j + 1) * LANES]
        y_ref[:, j * LANES:(j + 1) * LANES] = _gelu_tanh(y)


def _s5_prompt_body(u_ref, wb_ref, wc_ref, lr_ref, li_ref, d_ref, y_ref, sre_ref, sim_ref, re_ref, im_ref):
    @pl.when(pl.program_id(0) == 0)
    def _():
        sre_ref[...] = jnp.zeros_like(sre_ref)
        sim_ref[...] = jnp.zeros_like(sim_ref)

    u = u_ref[...]
    _s5_in_proj(u.astype(BF16), wb_ref, re_ref, im_ref)
    cw = 1024
    for cb in range(S5_STATES // cw):
        cols = slice(cb * cw, (cb + 1) * cw)
        lr = lr_ref[:, cols]
        li = li_ref[:, cols]

        def step(t, carry):
            xr, xi = carry
            rows = pl.ds(pl.multiple_of(t * SUBLANES, SUBLANES), SUBLANES)
            nr = lr * xr - li * xi + re_ref[rows, cols]
            ni = lr * xi + li * xr + im_ref[rows, cols]
            re_ref[rows, cols] = nr
            im_ref[rows, cols] = ni
            return nr, ni

        xr, xi = lax.fori_loop(0, S5_TT, step, (sre_ref[:, cols], sim_ref[:, cols]), unroll=2)
        sre_ref[:, cols] = xr
        sim_ref[:, cols] = xi
    _s5_out_proj(u, re_ref, im_ref, wc_ref, d_ref, y_ref)


def _s5_prompt(u_tm, wb, wc, lam_re8, lam_im8, d_row):
    rows = S5_TT * SUBLANES
    full = lambda shape: pl.BlockSpec(shape, lambda t: (0,) * len(shape))
    return pl.pallas_call(
        _s5_prompt_body,
        grid=(SEQ // S5_TT,),
        in_specs=[
            pl.BlockSpec((rows, D_MODEL), lambda t: (t, 0)),
            full((S5_NBLK, LANES, 2 * S5_BS)),
            full((S5_NBLK, 2 * S5_BS, LANES)),
            full((SUBLANES, S5_STATES)),
            full((SUBLANES, S5_STATES)),
            full((1, D_MODEL)),
        ],
        out_specs=[
            pl.BlockSpec((rows, D_MODEL), lambda t: (t, 0)),
            full((SUBLANES, S5_STATES)),
            full((SUBLANES, S5_STATES)),
        ],
        out_shape=[jax.ShapeDtypeStruct((SEQ * SUBLANES, D_MODEL), F32),
                   jax.ShapeDtypeStruct((SUBLANES, S5_STATES), F32),
                   jax.ShapeDtypeStruct((SUBLANES, S5_STATES), F32)],
        scratch_shapes=[pltpu.VMEM((rows, S5_STATES), F32), pltpu.VMEM((rows, S5_STATES), F32)],
        compiler_params=_cparams(("arbitrary",)),
        name="s5_prompt",
    )(u_tm, wb, wc, lam_re8, lam_im8, d_row)


def _s5_sample_body(u_ref, x0r_ref, x0i_ref, wb_ref, wc_ref, lr_ref, li_ref, d_ref, y_ref, x1r_ref, x1i_ref):
    u = u_ref[...]
    _s5_in_proj(u.astype(BF16), wb_ref, x1r_ref, x1i_ref)
    lr = lr_ref[...]
    li = li_ref[...]
    xr = x0r_ref[...]
    xi = x0i_ref[...]
    x1r_ref[...] = x1r_ref[...] + (lr * xr - li * xi)
    x1i_ref[...] = x1i_ref[...] + (lr * xi + li * xr)
    _s5_out_proj(u, x1r_ref, x1i_ref, wc_ref, d_ref, y_ref)


def _s5_sample(u_all, x0r, x0i, wb, wc, lam_re, lam_im, d_row, *, layer):
    full = lambda shape: pl.BlockSpec(shape, lambda i: (0,) * len(shape))
    st = pl.BlockSpec((None, DEC_BATCH, S5_STATES), lambda i: (layer, 0, 0))
    return pl.pallas_call(
        _s5_sample_body,
        grid=(1,),
        in_specs=[
            pl.BlockSpec((DEC_BATCH, D_MODEL), lambda i: (N_PROMPT // DEC_BATCH, 0)),
            st, st,
            full((S5_NBLK, LANES, 2 * S5_BS)),
            full((S5_NBLK, 2 * S5_BS, LANES)),
            full((1, S5_STATES)), full((1, S5_STATES)), full((1, D_MODEL)),
        ],
        out_specs=[full((DEC_BATCH, D_MODEL)), full((DEC_BATCH, S5_STATES)), full((DEC_BATCH, S5_STATES))],
        out_shape=[jax.ShapeDtypeStruct((DEC_BATCH, D_MODEL), F32),
                   jax.ShapeDtypeStruct((DEC_BATCH, S5_STATES), F32),
                   jax.ShapeDtypeStruct((DEC_BATCH, S5_STATES), F32)],
        compiler_params=_cparams(("arbitrary",)),
        name="s5_sample",
    )(u_all, x0r, x0i, wb, wc, lam_re, lam_im, d_row)


def _s5_params(lam_re, lam_im, log_dt, b_re, b_im, c_re, c_im, d_skip):
    lam = lax.complex(lam_re, lam_im)
    dt = jnp.exp(log_dt)[:, None]
    lam_bar = jnp.exp(lam * dt)
    b_bar = ((lam_bar - 1.0) / lam)[..., None] * lax.complex(b_re, b_im)
    eye = jnp.eye(S5_BLK, dtype=F32)

    def in_blocks(b):
        b = b.reshape(S5_NBLK, S5_BLK, S5_N, S5_P)
        return jnp.einsum("jgnp,gh->jgphn", b, eye).reshape(S5_NBLK, S5_BLK * S5_P, S5_BS)

    def out_blocks(c):
        c = c.reshape(S5_NBLK, S5_BLK, S5_P, S5_N)
        return jnp.einsum("jgpn,gh->jgnhp", c, eye).reshape(S5_NBLK, S5_BS, S5_BLK * S5_P)

    wb = jnp.concatenate([in_blocks(b_bar.real), in_blocks(b_bar.imag)], axis=2).astype(BF16)
    wc = jnp.concatenate([out_blocks(c_re), -out_blocks(c_im)], axis=1).astype(BF16)
    return (wb, wc, lam_bar.real.reshape(1, S5_STATES), lam_bar.imag.reshape(1, S5_STATES),
            d_skip.reshape(1, D_MODEL))


def _router_body(h_ref, w_ref, b_ref, idx_ref, wt_ref):
    logits = jnp.dot(h_ref[...].astype(BF16), w_ref[...].astype(BF16), preferred_element_type=F32)
    scores = _sigmoid(logits)
    biased = scores + b_ref[...]
    lane = lax.broadcasted_iota(jnp.int32, biased.shape, 1)
    kcol = lax.broadcasted_iota(jnp.int32, (biased.shape[0], TOP_K), 1)
    idx = jnp.zeros((biased.shape[0], TOP_K), jnp.int32)
    sel = jnp.zeros((biased.shape[0], TOP_K), F32)
    for k in range(TOP_K):
        m = jnp.max(biased, axis=-1, keepdims=True)
        first = jnp.min(jnp.where(biased == m, lane, N_EXPERTS), axis=-1, keepdims=True)
        hit = lane == first
        s = jnp.sum(jnp.where(hit, scores, 0.0), axis=-1, keepdims=True)
        idx = jnp.where(kcol == k, first, idx)
        sel = jnp.where(kcol == k, s, sel)
        biased = jnp.where(hit, -jnp.inf, biased)
    idx_ref[...] = idx
    wt_ref[...] = sel / jnp.sum(sel, axis=-1, keepdims=True) * ROUTED_SCALE


def _router(h, w_router, router_bias, *, layer):
    tm = MM_TM
    return pl.pallas_call(
        _router_body,
        grid=(M_ROWS // tm,),
        in_specs=[
            pl.BlockSpec((tm, D_MODEL), lambda i: (i, 0)),
            pl.BlockSpec((None, D_MODEL, N_EXPERTS), lambda i: (layer, 0, 0)),
            pl.BlockSpec((None, 1, N_EXPERTS), lambda i: (layer, 0, 0)),
        ],
        out_specs=[pl.BlockSpec((tm, TOP_K), lambda i: (i, 0))] * 2,
        out_shape=[jax.ShapeDtypeStruct((M_ROWS, TOP_K), jnp.int32),
                   jax.ShapeDtypeStruct((M_ROWS, TOP_K), F32)],
        compiler_params=_cparams(("arbitrary",)),
        name="router",
    )(h, w_router, router_bias.reshape(DEPTH, 1, N_EXPERTS))


def _routing_tables(top_idx):
    flat_e = top_idx.reshape(-1)
    order = jnp.argsort(flat_e)
    sorted_e = flat_e[order]
    sizes = jnp.bincount(flat_e, length=N_EXPERTS)
    starts = jnp.cumsum(sizes) - sizes
    padded = (sizes + MOE_BS - 1) // MOE_BS * MOE_BS
    pad_ends = jnp.cumsum(padded)
    pad_starts = pad_ends - padded
    dest = (pad_starts[sorted_e] + jnp.arange(N_ASSIGN) - starts[sorted_e]).astype(jnp.int32)
    row_tok = jnp.zeros((MOE_ROWS,), jnp.int32).at[dest].set((order // TOP_K).astype(jnp.int32))
    pos = jnp.zeros((N_ASSIGN,), jnp.int32).at[order].set(dest)
    blk_start = jnp.arange(MOE_BLOCKS) * MOE_BS
    block_e = jnp.minimum(jnp.searchsorted(pad_ends, blk_start, side="right"), N_EXPERTS - 1)
    active = (blk_start < pad_ends[-1]).astype(jnp.int32)
    return row_tok, pos, block_e.astype(jnp.int32), active


def _expert_gather(tok_ref, h_hbm, xbuf, sem, slot):
    def issue(r, carry):
        tok = tok_ref[0, r]
        pltpu.make_async_copy(h_hbm.at[pl.ds(tok, 1), :], xbuf.at[slot, pl.ds(r, 1), :], sem.at[slot]).start()
        return carry

    lax.fori_loop(0, MOE_BS, issue, 0, unroll=8)


def _experts_body(be_ref, act_ref, tok_ref, tokn_ref, h_hbm, wg_ref, wu_ref, wd_ref, o_ref,
                  xbuf, sem, wgb, wub, wdb):
    b = pl.program_id(0)
    nb = pl.num_programs(0)
    slot = lax.rem(b, 2)

    @pl.when(b == 0)
    def _():
        _expert_gather(tok_ref, h_hbm, xbuf, sem, 0)

    @pl.when(b + 1 < nb)
    def _():
        _expert_gather(tokn_ref, h_hbm, xbuf, sem, 1 - slot)

    changed = jnp.logical_or(b == 0, be_ref[b] != be_ref[jnp.maximum(b - 1, 0)])

    @pl.when(changed)
    def _():
        wgb[...] = wg_ref[...].astype(BF16)
        wub[...] = wu_ref[...].astype(BF16)
        wdb[...] = wd_ref[...].astype(BF16)

    pltpu.make_async_copy(h_hbm.at[pl.ds(0, MOE_BS), :], xbuf.at[slot], sem.at[slot]).wait()

    @pl.when(act_ref[b] == 1)
    def _():
        x = xbuf[slot].astype(BF16)
        g = jnp.dot(x, wgb[...], preferred_element_type=F32)
        u = jnp.dot(x, wub[...], preferred_element_type=F32)
        a = (_silu(g) * u).astype(BF16)
        o_ref[...] = jnp.dot(a, wdb[...], preferred_element_type=F32)

    @pl.when(act_ref[b] == 0)
    def _():
        o_ref[...] = jnp.zeros_like(o_ref)


def _experts(h, row_tok, block_e, active, w_gate, w_up, w_down, *, layer):
    tok3 = row_tok.reshape(MOE_BLOCKS, 1, MOE_BS)
    smem_tok = lambda f: pl.BlockSpec((None, 1, MOE_BS), f, memory_space=pltpu.SMEM)
    grid_spec = pltpu.PrefetchScalarGridSpec(
        num_scalar_prefetch=2,
        grid=(MOE_BLOCKS,),
        in_specs=[
            smem_tok(lambda b, be, act: (b, 0, 0)),
            smem_tok(lambda b, be, act: (jnp.minimum(b + 1, MOE_BLOCKS - 1), 0, 0)),
            pl.BlockSpec(memory_space=pl.ANY),
            pl.BlockSpec((None, None, D_MODEL, D_EXPERT), lambda b, be, act: (layer, be[b], 0, 0)),
            pl.BlockSpec((None, None, D_MODEL, D_EXPERT), lambda b, be, act: (layer, be[b], 0, 0)),
            pl.BlockSpec((None, None, D_EXPERT, D_MODEL), lambda b, be, act: (layer, be[b], 0, 0)),
        ],
        out_specs=pl.BlockSpec((MOE_BS, D_MODEL), lambda b, be, act: (b, 0)),
        scratch_shapes=[
            pltpu.VMEM((2, MOE_BS, D_MODEL), F32),
            pltpu.SemaphoreType.DMA((2,)),
            pltpu.VMEM((D_MODEL, D_EXPERT), BF16),
            pltpu.VMEM((D_MODEL, D_EXPERT), BF16),
            pltpu.VMEM((D_EXPERT, D_MODEL), BF16),
        ],
    )
    return pl.pallas_call(
        _experts_body,
        grid_spec=grid_spec,
        out_shape=jax.ShapeDtypeStruct((MOE_ROWS, D_MODEL), F32),
        compiler_params=_cparams(("arbitrary",)),
        name="experts",
    )(block_e, active, tok3, tok3, h, w_gate, w_up, w_down)


def _shared_body(h_ref, wg_ref, wu_ref, wd_ref, o_ref, wgb, wub, wdb):
    @pl.when(pl.program_id(0) == 0)
    def _():
        wgb[...] = wg_ref[...].astype(BF16)
        wub[...] = wu_ref[...].astype(BF16)
        wdb[...] = wd_ref[...].astype(BF16)

    x = h_ref[...].astype(BF16)
    g = jnp.dot(x, wgb[...], preferred_element_type=F32)
    u = jnp.dot(x, wub[...], preferred_element_type=F32)
    a = (_silu(g) * u).astype(BF16)
    o_ref[...] = jnp.dot(a, wdb[...], preferred_element_type=F32)


def _shared_expert(h, w_gate, w_up, w_down, *, layer):
    tm = MM_TM
    return pl.pallas_call(
        _shared_body,
        grid=(M_ROWS // tm,),
        in_specs=[
            pl.BlockSpec((tm, D_MODEL), lambda i: (i, 0)),
            pl.BlockSpec((None, D_MODEL, D_EXPERT), lambda i: (layer, 0, 0)),
            pl.BlockSpec((None, D_MODEL, D_EXPERT), lambda i: (layer, 0, 0)),
            pl.BlockSpec((None, D_EXPERT, D_MODEL), lambda i: (layer, 0, 0)),
        ],
        out_specs=pl.BlockSpec((tm, D_MODEL), lambda i: (i, 0)),
        out_shape=jax.ShapeDtypeStruct((M_ROWS, D_MODEL), F32),
        scratch_shapes=[pltpu.VMEM((D_MODEL, D_EXPERT), BF16), pltpu.VMEM((D_MODEL, D_EXPERT), BF16),
                        pltpu.VMEM((D_EXPERT, D_MODEL), BF16)],
        compiler_params=_cparams(("arbitrary",)),
        name="shared_expert",
    )(h, w_gate, w_up, w_down)


def _combine_body(pos_ref, y_hbm, wt_ref, sh_ref, o_ref, buf, sem):
    def issue(a, carry):
        p = pos_ref[0, a]
        t = a // TOP_K
        k = a - t * TOP_K
        pltpu.make_async_copy(y_hbm.at[pl.ds(p, 1), :], buf.at[k, pl.ds(t, 1), :], sem.at[0]).start()
        return carry

    lax.fori_loop(0, COMB_T * TOP_K, issue, 0, unroll=8)
    for k in range(TOP_K):
        pltpu.make_async_copy(y_hbm.at[pl.ds(0, COMB_T), :], buf.at[k], sem.at[0]).wait()
    wt = wt_ref[...]
    acc = sh_ref[...]
    for k in range(TOP_K):
        acc = acc + buf[k] * wt[:, k:k + 1]
    o_ref[...] = acc


def _combine(y_sorted, pos, top_w, shared):
    nt = M_ROWS // COMB_T
    pos3 = pos.reshape(nt, 1, COMB_T * TOP_K)
    return pl.pallas_call(
        _combine_body,
        grid=(nt,),
        in_specs=[
            pl.BlockSpec((None, 1, COMB_T * TOP_K), lambda i: (i, 0, 0), memory_space=pltpu.SMEM),
            pl.BlockSpec(memory_space=pl.ANY),
            pl.BlockSpec((COMB_T, TOP_K), lambda i: (i, 0)),
            pl.BlockSpec((COMB_T, D_MODEL), lambda i: (i, 0)),
        ],
        out_specs=pl.BlockSpec((COMB_T, D_MODEL), lambda i: (i, 0)),
        out_shape=jax.ShapeDtypeStruct((M_ROWS, D_MODEL), F32),
        scratch_shapes=[pltpu.VMEM((TOP_K, COMB_T, D_MODEL), F32), pltpu.SemaphoreType.DMA((1,))],
        compiler_params=_cparams(("arbitrary",)),
        name="combine",
    )(pos3, y_sorted, top_w, shared)


def _mixer(l, h1, p, cache_t, state_gdn, s5_x0r, s5_x0i):
    w_in = p["w_in"]
    qkv = _mm(h1, w_in, layer=l, col0=0, n=QKV_W, tn=1024, name="in_qkv")
    z = _mm(h1, w_in, layer=l, col0=COL_Z, n=V_W, tn=1024, name="in_z")
    ba = _mm(h1, p["w_ba"], layer=l, col0=0, n=LANES, tn=LANES, name="in_ba")
    tail = _mm(h1, p["w_tail"], layer=l, col0=0, n=3 * D_MODEL, tn=1024, name="in_tail")

    gb, gcum = _gates(ba, p["alog_row"][l], p["dtb_row"][l])
    to_heads = lambda x: x.reshape(BATCH, N_CHUNKS, CHUNK, GDN_GROUPS, GDN_VH).transpose(0, 1, 3, 2, 4)
    bcol = to_heads(gb[:N_PROMPT, :V_HEADS])
    gcol = to_heads(gcum[:N_PROMPT, V_HEADS:2 * V_HEADS])
    grow = gcol.transpose(0, 1, 2, 4, 3)
    o_p, gdn_p = _gdn_prompt(qkv, z, p["conv_w"], gcol, grow, bcol, p["gdn_norm_w"], layer=l)

    qk_s = _gdn_sample_prep(qkv, cache_t, p["conv_w"], layer=l, col0=0, n=2 * QK_W, normalize=True)
    v_s = _gdn_sample_prep(qkv, cache_t, p["conv_w"], layer=l, col0=2 * QK_W, n=V_W, normalize=False)
    col_form = lambda x: x.reshape(DEC_BATCH, QK_HEADS, HEAD_DIM).transpose(0, 2, 1)
    qt = col_form(qk_s[:, :QK_W])
    kt = col_form(qk_s[:, QK_W:])
    beta_s = gb[N_PROMPT:, :V_HEADS].reshape(DEC_BATCH, 1, V_HEADS)
    eg_s = jnp.exp(gb[N_PROMPT:, V_HEADS:2 * V_HEADS]).reshape(DEC_BATCH, 1, V_HEADS)
    z_s = z[N_PROMPT:].reshape(DEC_BATCH, 1, V_W)
    gdn_s, o_s = _gdn_sample(state_gdn, qt, kt, v_s.reshape(DEC_BATCH, 1, V_W), z_s, eg_s, beta_s,
                             p["gdn_norm_w"], layer=l)
    o_all = jnp.concatenate([o_p, o_s.reshape(DEC_BATCH, V_W)], axis=0)
    y_a = _mm(o_all, p["w_br_a"], layer=l, col0=0, n=D_MODEL, tn=512, name="br_a")

    wb, wc, lam_re, lam_im, d_row = p["s5"][l]
    u_p = tail[:N_PROMPT, :D_MODEL].reshape(BATCH, SEQ, D_MODEL).transpose(1, 0, 2)
    u_tm = jnp.pad(u_p, ((0, 0), (0, SUBLANES - BATCH), (0, 0))).reshape(SEQ * SUBLANES, D_MODEL)
    lam_re8 = jnp.broadcast_to(lam_re, (SUBLANES, S5_STATES))
    lam_im8 = jnp.broadcast_to(lam_im, (SUBLANES, S5_STATES))
    y_tm, re_p, im_p = _s5_prompt(u_tm, wb, wc, lam_re8, lam_im8, d_row)
    y_p = y_tm.reshape(SEQ, SUBLANES, D_MODEL)[:, :BATCH].transpose(1, 0, 2).reshape(N_PROMPT, D_MODEL)
    y_s, re_s, im_s = _s5_sample(tail, s5_x0r, s5_x0i, wb, wc, lam_re, lam_im, d_row, layer=l)
    y_all = jnp.concatenate([y_p, y_s], axis=0)
    ys_spec = pl.BlockSpec((MM_TM, 1024), lambda j, i: (i, j))
    y_glu = _mm(y_all, p["w_glu"], layer=l, col0=0, n=D_MODEL, tn=1024, out_dtype=BF16, epilogue="glu",
                extra=(y_all,), extra_specs=(ys_spec,), name="glu")

    tn = 1024
    nb = D_MODEL // tn
    merged = _mm(y_glu, p["w_br_b"], layer=l, col0=0, n=D_MODEL, tn=tn, out_dtype=BF16, epilogue="merge",
                 extra=(y_a, tail, tail),
                 extra_specs=(pl.BlockSpec((MM_TM, tn), lambda j, i: (i, j)),
                              pl.BlockSpec((MM_TM, tn), lambda j, i: (i, nb + j)),
                              pl.BlockSpec((MM_TM, tn), lambda j, i: (i, 2 * nb + j))),
                 name="br_b_merge")
    mix = _mm(merged, p["w_out"], layer=l, col0=0, n=D_MODEL, tn=1024, name="out_proj")

    conv_p = qkv[:N_PROMPT].reshape(BATCH, SEQ, QKV_W)[:, SEQ - (CONV_W - 1):]
    conv_s = jnp.concatenate([cache_t[l, 1:].transpose(1, 0, 2), qkv[N_PROMPT:, None, :]], axis=1)
    states = (conv_p, gdn_p,
              re_p[:BATCH].reshape(BATCH, S5_G, S5_N), im_p[:BATCH].reshape(BATCH, S5_G, S5_N),
              conv_s, gdn_s,
              re_s.reshape(DEC_BATCH, S5_G, S5_N), im_s.reshape(DEC_BATCH, S5_G, S5_N))
    return mix, states


def _moe(l, h2, p):
    top_idx, top_w = _router(h2, p["w_router"], p["router_bias"], layer=l)
    row_tok, pos, block_e, active = _routing_tables(top_idx)
    y_sorted = _experts(h2, row_tok, block_e, active, p["w_exp_gate"], p["w_exp_up"], p["w_exp_down"], layer=l)
    shared = _shared_expert(h2, p["w_sh_gate"], p["w_sh_up"], p["w_sh_down"], layer=l)
    return _combine(y_sorted, pos, top_w, shared)


def kernel(x_prompt, x_sample, cache_conv, state_gdn, state_s5_re, state_s5_im, c_prompt, c_sample, ln_in_g, ln_in_b, w_ada, b_ada, w_in, conv_w, gdn_a_log, gdn_dt_bias, gdn_norm_w, s5_lam_re, s5_lam_im, s5_log_dt, s5_b_re, s5_b_im, s5_c_re, s5_c_im, s5_d, w_glu, w_br_a, w_br_b, w_out, ln1_g, ln1_b, w_router, router_bias, w_exp_gate, w_exp_up, w_exp_down, w_sh_gate, w_sh_up, w_sh_down, ln2_g, ln2_b):
    c_all = jnp.concatenate([c_prompt, jnp.zeros((SUBLANES - BATCH, D_MODEL), F32), c_sample], axis=0)
    mod = _ada_all(c_all, w_ada, b_ada)
    mod_p = mod[:, :SUBLANES]
    mod_s = mod[:, SUBLANES:]

    lane_pad = lambda a: jnp.pad(a, ((0, 0), (V_HEADS, LANES - 2 * V_HEADS)))[:, None, :]
    p = {
        "w_in": w_in,
        "w_ba": jnp.pad(w_in[:, :, COL_BA:COL_TAIL], ((0, 0), (0, 0), (0, LANES - 2 * V_HEADS))),
        "w_tail": w_in[:, :, COL_TAIL:],
        "alog_row": lane_pad(gdn_a_log), "dtb_row": lane_pad(gdn_dt_bias),
        "conv_w": conv_w, "gdn_norm_w": gdn_norm_w,
        "s5": [_s5_params(s5_lam_re[l], s5_lam_im[l], s5_log_dt[l], s5_b_re[l], s5_b_im[l],
                          s5_c_re[l], s5_c_im[l], s5_d[l]) for l in range(DEPTH)],
        "w_glu": w_glu, "w_br_a": w_br_a, "w_br_b": w_br_b, "w_out": w_out,
        "w_router": w_router, "router_bias": router_bias,
        "w_exp_gate": w_exp_gate, "w_exp_up": w_exp_up, "w_exp_down": w_exp_down,
        "w_sh_gate": w_sh_gate, "w_sh_up": w_sh_up, "w_sh_down": w_sh_down,
    }
    cache_t = cache_conv.transpose(0, 2, 1, 3)
    s5_x0r = state_s5_re.reshape(DEPTH, DEC_BATCH, S5_STATES)
    s5_x0i = state_s5_im.reshape(DEPTH, DEC_BATCH, S5_STATES)

    x_all = jnp.concatenate([x_prompt.reshape(N_PROMPT, D_MODEL), x_sample.reshape(DEC_BATCH, D_MODEL)], axis=0)
    x, h1 = _ln_in(x_all, ln_in_g, ln_in_b, mod_p, mod_s)

    per_layer = []
    for l in range(DEPTH):
        mix, states = _mixer(l, h1, p, cache_t, state_gdn, s5_x0r, s5_x0i)
        per_layer.append(states)
        x, h2 = _deepnorm(x, mix, ln1_g, ln1_b, mod_p, mod_s, layer=l, gate_comp=2, mod_layer=l,
                          sc_comp=4, sh_comp=3, h_dtype=F32)
        ffn = _moe(l, h2, p)
        nxt = min(l + 1, DEPTH - 1)
        x, h1 = _deepnorm(x, ffn, ln2_g, ln2_b, mod_p, mod_s, layer=l, gate_comp=5, mod_layer=nxt,
                          sc_comp=1, sh_comp=0, h_dtype=BF16)

    stacked = [jnp.stack([s[i] for s in per_layer]) for i in range(8)]
    return (x[:N_PROMPT].reshape(BATCH, SEQ, D_MODEL), x[N_PROMPT:].reshape(DEC_BATCH, 1, D_MODEL), *stacked)
```

```python
import functools
import math

import jax
import jax.numpy as jnp
from jax import lax
from jax.experimental import pallas as pl
from jax.experimental.pallas import tpu as pltpu

F32 = jnp.float32
BF16 = jnp.bfloat16

D_MODEL = 2048
BATCH = 4
SEQ = 2048
DEPTH = 4
DEC_BATCH = 128
N_PROMPT = BATCH * SEQ
M_ROWS = N_PROMPT + DEC_BATCH

QK_HEADS = 16
V_HEADS = 32
HEAD_DIM = 128
QK_W = QK_HEADS * HEAD_DIM
V_W = V_HEADS * HEAD_DIM
QKV_W = 2 * QK_W + V_W
CONV_W = 4
CHUNK = 64
N_CHUNKS = SEQ // CHUNK

S5_P = 16
S5_N = 64
S5_G = D_MODEL // S5_P
S5_STATES = S5_G * S5_N

N_EXPERTS = 64
TOP_K = 8
D_EXPERT = D_MODEL // 4
ROUTED_SCALE = 2.5
N_ASSIGN = M_ROWS * TOP_K

ALPHA = (2 * DEPTH) ** 0.25
EPS = 1e-6

COL_Z = QKV_W
COL_BA = QKV_W + V_W
COL_TAIL = COL_BA + 2 * V_HEADS

LANES = 128
SUBLANES = 8
VMEM_LIMIT = 56 * 1024 * 1024

ROW_TILE = 128
N_ROW_TILES = M_ROWS // ROW_TILE
PROMPT_ROW_TILES = N_PROMPT // ROW_TILE
TILES_PER_SEQ = SEQ // ROW_TILE
MM_TM = 640

QUAD = 4 * CHUNK
GDN_QUADS = 4
GDN_PAIRS = 2 * GDN_QUADS
GDN_GROUPS = QK_HEADS // GDN_PAIRS
GDN_VH = 2 * GDN_PAIRS

S5_TT = 32
S5_BLK = 8
S5_NBLK = S5_G // S5_BLK
S5_BS = S5_BLK * S5_N

ROW_SLAB = D_MODEL // LANES
SLAB_PITCH = ROW_SLAB + 4
MOE_BS = 256
MOE_BLOCKS = -(-N_ASSIGN // MOE_BS) + N_EXPERTS
MOE_ROWS = MOE_BLOCKS * MOE_BS
COMB_T = 128


def _cparams(sem, vmem=VMEM_LIMIT):
    return pltpu.CompilerParams(dimension_semantics=sem, vmem_limit_bytes=vmem)


def _sigmoid(x):
    return jax.nn.sigmoid(x)


def _silu(x):
    return x * jax.nn.sigmoid(x)


def _gelu_tanh(x):
    c = math.sqrt(2.0 / math.pi)
    return 0.5 * x * (1.0 + jnp.tanh(c * (x + 0.044715 * (x * x * x))))


def _layer_norm(x, g, b):
    mu = jnp.mean(x, axis=-1, keepdims=True)
    xc = x - mu
    var = jnp.mean(xc * xc, axis=-1, keepdims=True)
    return xc * lax.rsqrt(var + EPS) * g + b


def _mm_body(*refs, prologue, epilogue, n_extra):
    a_ref, w_ref = refs[0], refs[1]
    extra = refs[2:2 + n_extra]
    o_ref = refs[2 + n_extra]
    wbf_ref = refs[3 + n_extra]

    @pl.when(pl.program_id(1) == 0)
    def _():
        wbf_ref[...] = w_ref[...].astype(BF16)

    a = a_ref[...]
    if prologue == "silu":
        a = _silu(a.astype(F32))
    acc = jnp.dot(a.astype(BF16), wbf_ref[...], preferred_element_type=F32)
    if epilogue == "bias":
        acc = acc + extra[0][...]
    elif epilogue == "glu":
        acc = extra[0][...].astype(F32) * _sigmoid(acc)
    elif epilogue == "merge":
        ya, ga, gb = extra[0][...], extra[1][...], extra[2][...]
        acc = _sigmoid(ga) * ya + _sigmoid(gb) * acc
    o_ref[...] = acc.astype(o_ref.dtype)


def _mm(a, w, *, layer, col0, n, tn, tm=MM_TM, out_dtype=F32, prologue=None, epilogue=None,
        extra=(), extra_specs=(), rows=None, name="mm"):
    rows = a.shape[0] if rows is None else rows
    k = a.shape[1]
    assert rows % tm == 0 and n % tn == 0 and col0 % tn == 0
    cb0 = col0 // tn
    grid = (n // tn, rows // tm)
    in_specs = [
        pl.BlockSpec((tm, k), lambda j, i: (i, 0)),
        pl.BlockSpec((None, k, tn), lambda j, i: (layer, 0, cb0 + j)),
    ] + list(extra_specs)
    return pl.pallas_call(
        functools.partial(_mm_body, prologue=prologue, epilogue=epilogue, n_extra=len(extra)),
        grid=grid,
        in_specs=in_specs,
        out_specs=pl.BlockSpec((tm, tn), lambda j, i: (i, j)),
        out_shape=jax.ShapeDtypeStruct((rows, n), out_dtype),
        scratch_shapes=[pltpu.VMEM((k, tn), BF16)],
        compiler_params=_cparams(("arbitrary", "arbitrary")),
        name=name,
    )(a, w, *extra)


def _ada_body(c_ref, w_ref, b_ref, o_ref):
    a = _silu(c_ref[...]).astype(BF16)
    o_ref[...] = jnp.dot(a, w_ref[...].astype(BF16), preferred_element_type=F32) + b_ref[...]


def _ada_all(c_all, w_ada, b_ada):
    rows = c_all.shape[0]
    tn = 1024
    nj = (6 * D_MODEL) // tn
    return pl.pallas_call(
        _ada_body,
        grid=(DEPTH * nj,),
        in_specs=[
            pl.BlockSpec((rows, D_MODEL), lambda g: (0, 0)),
            pl.BlockSpec((None, D_MODEL, tn), lambda g: (g // nj, 0, g % nj)),
            pl.BlockSpec((None, 1, tn), lambda g: (g // nj, 0, g % nj)),
        ],
        out_specs=pl.BlockSpec((None, rows, tn), lambda g: (g // nj, 0, g % nj)),
        out_shape=jax.ShapeDtypeStruct((DEPTH, rows, 6 * D_MODEL), F32),
        compiler_params=_cparams(("arbitrary",)),
        name="ada",
    )(c_all, w_ada, b_ada.reshape(DEPTH, 1, 6 * D_MODEL))


def _mod_pick(i, mp_ref, ms_ref):
    b = jnp.minimum(i // TILES_PER_SEQ, BATCH - 1)
    row = mp_ref[pl.ds(b, 1), :]
    return jnp.where(i < PROMPT_ROW_TILES, row, ms_ref[...])


def _mod_specs(layer, comp):
    return [
        pl.BlockSpec((None, SUBLANES, D_MODEL), lambda i: (layer, 0, comp)),
        pl.BlockSpec((None, DEC_BATCH, D_MODEL), lambda i: (layer, 0, comp)),
    ]


def _row_spec():
    return pl.BlockSpec((ROW_TILE, D_MODEL), lambda i: (i, 0))


def _vec_spec():
    return pl.BlockSpec((1, D_MODEL), lambda i: (0, 0))


def _ln_in_body(x_ref, g_ref, b_ref, scp, scs, shp, shs, x_out, h_out):
    i = pl.program_id(0)
    x = _layer_norm(x_ref[...], g_ref[...], b_ref[...])
    x_out[...] = x
    h = x * (1.0 + _mod_pick(i, scp, scs)) + _mod_pick(i, shp, shs)
    h_out[...] = h.astype(h_out.dtype)


def _ln_in(x_all, g, b, mod_p, mod_s):
    return pl.pallas_call(
        _ln_in_body,
        grid=(N_ROW_TILES,),
        in_specs=[_row_spec(), _vec_spec(), _vec_spec()] + _mod_specs(0, 1) + _mod_specs(0, 0),
        out_specs=[_row_spec(), _row_spec()],
        out_shape=[jax.ShapeDtypeStruct((M_ROWS, D_MODEL), F32),
                   jax.ShapeDtypeStruct((M_ROWS, D_MODEL), BF16)],
        compiler_params=_cparams(("arbitrary",)),
        name="ln_in",
    )(x_all, g.reshape(1, -1), b.reshape(1, -1), mod_p, mod_s, mod_p, mod_s)


def _deepnorm_body(x_ref, y_ref, g_ref, b_ref, gtp, gts, scp, scs, shp, shs, x_out, h_out):
    i = pl.program_id(0)
    r = ALPHA * x_ref[...] + _mod_pick(i, gtp, gts) * y_ref[...]
    x = _layer_norm(r, g_ref[...], b_ref[...])
    x_out[...] = x
    h = x * (1.0 + _mod_pick(i, scp, scs)) + _mod_pick(i, shp, shs)
    h_out[...] = h.astype(h_out.dtype)


def _deepnorm(x, y, g, b, mod_p, mod_s, *, layer, gate_comp, mod_layer, sc_comp, sh_comp, h_dtype):
    return pl.pallas_call(
        _deepnorm_body,
        grid=(N_ROW_TILES,),
        in_specs=[_row_spec(), _row_spec(),
                  pl.BlockSpec((None, 1, D_MODEL), lambda i: (layer, 0, 0)),
                  pl.BlockSpec((None, 1, D_MODEL), lambda i: (layer, 0, 0))]
        + _mod_specs(layer, gate_comp) + _mod_specs(mod_layer, sc_comp) + _mod_specs(mod_layer, sh_comp),
        out_specs=[_row_spec(), _row_spec()],
        out_shape=[jax.ShapeDtypeStruct((M_ROWS, D_MODEL), F32),
                   jax.ShapeDtypeStruct((M_ROWS, D_MODEL), h_dtype)],
        compiler_params=_cparams(("arbitrary",)),
        name="deepnorm",
    )(x, y, g.reshape(DEPTH, 1, D_MODEL), b.reshape(DEPTH, 1, D_MODEL),
      mod_p, mod_s, mod_p, mod_s, mod_p, mod_s)


def _gates_body(ba_ref, alog_ref, dtb_ref, gb_ref, gc_ref):
    x = ba_ref[...]
    lane = lax.broadcasted_iota(jnp.int32, x.shape, 1)
    beta = _sigmoid(x)
    z = x + dtb_ref[...]
    softplus = jnp.maximum(z, 0.0) + jnp.log1p(jnp.exp(-jnp.abs(z)))
    g = -jnp.exp(alog_ref[...]) * softplus
    is_beta = lane < V_HEADS
    is_g = jnp.logical_and(lane >= V_HEADS, lane < 2 * V_HEADS)
    g = jnp.where(is_g, g, 0.0)
    gb_ref[...] = jnp.where(is_beta, beta, g)
    r = lax.broadcasted_iota(jnp.int32, (ROW_TILE, ROW_TILE), 0)
    c = lax.broadcasted_iota(jnp.int32, (ROW_TILE, ROW_TILE), 1)
    tri = jnp.logical_and(c <= r, (r // CHUNK) == (c // CHUNK)).astype(F32)
    gc_ref[...] = jnp.dot(tri, g, preferred_element_type=F32, precision=lax.Precision.HIGHEST)


def _gates(ba, alog_row, dtb_row):
    spec = pl.BlockSpec((ROW_TILE, LANES), lambda i: (i, 0))
    vec = pl.BlockSpec((1, LANES), lambda i: (0, 0))
    return pl.pallas_call(
        _gates_body,
        grid=(N_ROW_TILES,),
        in_specs=[spec, vec, vec],
        out_specs=[spec, spec],
        out_shape=[jax.ShapeDtypeStruct((M_ROWS, LANES), F32)] * 2,
        compiler_params=_cparams(("arbitrary",)),
        name="gates",
    )(ba, alog_row, dtb_row)


def _dotb(x, y):
    return jnp.dot(x.astype(BF16), y.astype(BF16), preferred_element_type=F32)


def _inv_masks():
    r = jnp.arange(QUAD)[:, None]
    c = jnp.arange(QUAD)[None, :]
    same = lambda sh: (r >> sh) == (c >> sh)
    masks = [same(3)] + [jnp.logical_and(same(sh), jnp.logical_not(same(sh - 1))) for sh in (4, 5, 6)] + [r == c]
    return jnp.stack(masks).astype(BF16)


def _inv_unit_lower(mats, m_ref):
    ds = [a * m_ref[0] for a in mats]
    d2s = [_dotb(d, d).astype(BF16) for d in ds]
    d4s = [_dotb(d2, d2).astype(BF16) for d2 in d2s]
    ts = [m_ref[4].astype(F32) - d.astype(F32) for d in ds]
    ts = [t + _dotb(t, d2) for t, d2 in zip(ts, d2s)]
    ts = [t + _dotb(t, d4) for t, d4 in zip(ts, d4s)]
    for lvl in (1, 2, 3):
        tbs = [t.astype(BF16) for t in ts]
        ets = [_dotb(a * m_ref[lvl], tb).astype(BF16) for a, tb in zip(mats, tbs)]
        ts = [t - _dotb(tb, et) for t, tb, et in zip(ts, tbs, ets)]
    return ts


def _conv_silu(x_ref, hist_ref, w_ref):
    x = x_ref[...]
    w = w_ref[...]
    xc = jnp.concatenate([hist_ref[...], x], axis=0)
    base = SUBLANES - (CONV_W - 1)
    y = xc[base:base + CHUNK] * w[0:1]
    for j in range(1, CONV_W - 1):
        y = y + xc[base + j:base + j + CHUNK] * w[j:j + 1]
    y = y + x * w[CONV_W - 1:CONV_W]
    hist_ref[...] = x[CHUNK - SUBLANES:CHUNK]
    return _silu(y)


def _gdn_prompt_body(q_ref, k_ref, v_ref, wq_ref, wk_ref, wv_ref, z_ref, gates_ref, nw_ref, m_ref, o_ref, s_ref,
                     hq_ref, hk_ref, hv_ref):
    c = pl.program_id(2)

    @pl.when(c == 0)
    def _():
        s_ref[...] = jnp.zeros_like(s_ref)
        hq_ref[...] = jnp.zeros_like(hq_ref)
        hk_ref[...] = jnp.zeros_like(hk_ref)
        hv_ref[...] = jnp.zeros_like(hv_ref)

    s_old = [s_ref[j] for j in range(GDN_VH)]
    qa = _conv_silu(q_ref, hq_ref, wq_ref)
    ka = _conv_silu(k_ref, hk_ref, wk_ref)
    va = _conv_silu(v_ref, hv_ref, wv_ref)
    z = z_ref[...]
    nw = nw_ref[...]
    head = lambda x, j: x[:, j * HEAD_DIM:(j + 1) * HEAD_DIM]

    pr = 2 * CHUNK
    row = lax.broadcasted_iota(jnp.int32, (pr, pr), 0)
    col = lax.broadcasted_iota(jnp.int32, (pr, pr), 1)
    same_head = (row >> 6) == (col >> 6)
    causal = jnp.logical_and(same_head, row >= col)
    strict = jnp.logical_and(same_head, row > col)
    zero_b = jnp.zeros((pr, pr), BF16)

    blockdiag = lambda x: jnp.concatenate([jnp.concatenate([x[0], zero_b], axis=1),
                                           jnp.concatenate([zero_b, x[1]], axis=1)], axis=0)
    quads = range(GDN_QUADS)
    hrows = lambda x, h: x[h * CHUNK:(h + 1) * CHUNK]

    gcol, bcol, glcol, eg, kst, qst, vst, kts, a_mat, qkd = [], [], [], [], [], [], [], [], [], []
    for qd in quads:
        gates = gates_ref[qd]
        gcols = gates.T
        gcol.append(gcols[:, 0:1])
        bcol.append(gcols[:, 1:2])
        glcol.append(gcols[:, 2:3])
        grow = gates[0:1, :]
        eg.append(jnp.exp(gcol[qd]))
        kps, qps, kpts, a_blk, qkd_blk = [], [], [], [], []
        for pp in range(2):
            q = head(qa, 2 * qd + pp)
            k = head(ka, 2 * qd + pp)
            qn = q * lax.rsqrt(jnp.sum(q * q, axis=-1, keepdims=True) + EPS) * (HEAD_DIM ** -0.5)
            kn = k * lax.rsqrt(jnp.sum(k * k, axis=-1, keepdims=True) + EPS)
            kp = jnp.concatenate([kn, kn], axis=0)
            qp = jnp.concatenate([qn, qn], axis=0)
            kp_t = kp.T.astype(BF16)
            rs = slice(pp * pr, (pp + 1) * pr)
            dec = jnp.exp(jnp.where(causal, gcol[qd][rs] - grow[:, rs], -jnp.inf))
            a_blk.append(jnp.where(strict, bcol[qd][rs] * _dotb(kp, kp_t) * dec, 0.0).astype(BF16))
            qkd_blk.append((_dotb(qp, kp_t) * dec).astype(BF16))
            kps.append(kp)
            qps.append(qp)
            kpts.append(kp_t)
        kst.append(jnp.concatenate(kps, axis=0))
        qst.append(jnp.concatenate(qps, axis=0))
        vst.append(jnp.concatenate([head(va, 4 * qd + h) for h in range(4)], axis=0))
        kts.append(kpts)
        a_mat.append(blockdiag(a_blk))
        qkd.append(blockdiag(qkd_blk))

    t = _inv_unit_lower(a_mat, m_ref)
    sol = [_dotb(t[qd], jnp.concatenate([vst[qd] * bcol[qd], kst[qd] * (bcol[qd] * eg[qd])], axis=1)) for qd in quads]
    qdec = [qst[qd] * eg[qd] for qd in quads]
    res = [[_dotb(jnp.concatenate([hrows(sol[qd][:, HEAD_DIM:], h), hrows(qdec[qd], h)], axis=0), s_old[4 * qd + h])
            for qd in quads] for h in range(4)]
    v_new = [jnp.concatenate([hrows(sol[qd][:, :HEAD_DIM], h) - res[h][qd][:CHUNK] for h in range(4)], axis=0)
             for qd in quads]
    o = [jnp.concatenate([res[h][qd][CHUNK:] for h in range(4)], axis=0) + _dotb(qkd[qd], v_new[qd]) for qd in quads]
    v_sc = [v_new[qd] * jnp.exp(glcol[qd] - gcol[qd]) for qd in quads]
    upd = [[_dotb(kts[qd][pp][:, :CHUNK],
                  jnp.concatenate([hrows(v_sc[qd], 2 * pp), hrows(v_sc[qd], 2 * pp + 1)], axis=1))
            for qd in quads] for pp in range(2)]
    for qd in quads:
        for h in range(4):
            j = 4 * qd + h
            oh = hrows(o[qd], h)
            on = oh * lax.rsqrt(jnp.mean(oh * oh, axis=-1, keepdims=True) + EPS) * nw * _silu(head(z, j))
            o_ref[:, j * HEAD_DIM:(j + 1) * HEAD_DIM] = on.astype(o_ref.dtype)
    for qd in quads:
        for h in range(4):
            gl = glcol[qd][h * CHUNK:h * CHUNK + 1, :]
            s_ref[4 * qd + h] = s_old[4 * qd + h] * jnp.exp(gl) + head(upd[h // 2][qd], h % 2)


def _gdn_gate_rows(gb, gcum):
    nq = V_HEADS // 4
    to_quads = lambda x: x.reshape(BATCH, N_CHUNKS, CHUNK, nq, 4).transpose(0, 1, 3, 4, 2)
    g = to_quads(gcum[:N_PROMPT, V_HEADS:2 * V_HEADS])
    beta = to_quads(gb[:N_PROMPT, :V_HEADS])
    glast = jnp.broadcast_to(g[..., CHUNK - 1:], g.shape)
    rows = jnp.stack([g, beta, glast], axis=3).reshape(BATCH, N_CHUNKS, nq, 3, QUAD)
    return jnp.pad(rows, ((0, 0), (0, 0), (0, 0), (0, SUBLANES - 3), (0, 0)))


def _gdn_prompt(qkv, z, conv_w, gates, norm_w, *, layer):
    qw = GDN_PAIRS * HEAD_DIM
    vw = GDN_VH * HEAD_DIM
    kb0 = QK_W // qw
    vb0 = 2 * QK_W // vw
    rowblk = lambda b, h, c: b * N_CHUNKS + c
    return pl.pallas_call(
        _gdn_prompt_body,
        grid=(BATCH, GDN_GROUPS, N_CHUNKS),
        in_specs=[
            pl.BlockSpec((CHUNK, qw), lambda b, h, c: (rowblk(b, h, c), h)),
            pl.BlockSpec((CHUNK, qw), lambda b, h, c: (rowblk(b, h, c), kb0 + h)),
            pl.BlockSpec((CHUNK, vw), lambda b, h, c: (rowblk(b, h, c), vb0 + h)),
            pl.BlockSpec((None, CONV_W, qw), lambda b, h, c: (layer, 0, h)),
            pl.BlockSpec((None, CONV_W, qw), lambda b, h, c: (layer, 0, kb0 + h)),
            pl.BlockSpec((None, CONV_W, vw), lambda b, h, c: (layer, 0, vb0 + h)),
            pl.BlockSpec((CHUNK, vw), lambda b, h, c: (rowblk(b, h, c), h)),
            pl.BlockSpec((None, None, GDN_QUADS, SUBLANES, QUAD), lambda b, h, c: (b, c, h, 0, 0)),
            pl.BlockSpec((None, 1, HEAD_DIM), lambda b, h, c: (layer, 0, 0)),
            pl.BlockSpec((5, QUAD, QUAD), lambda b, h, c: (0, 0, 0)),
        ],
        out_specs=[
            pl.BlockSpec((CHUNK, vw), lambda b, h, c: (rowblk(b, h, c), h)),
            pl.BlockSpec((None, GDN_VH, HEAD_DIM, HEAD_DIM), lambda b, h, c: (b, h, 0, 0)),
        ],
        out_shape=[jax.ShapeDtypeStruct((N_PROMPT, V_W), BF16),
                   jax.ShapeDtypeStruct((BATCH, V_HEADS, HEAD_DIM, HEAD_DIM), F32)],
        scratch_shapes=[pltpu.VMEM((SUBLANES, qw), F32), pltpu.VMEM((SUBLANES, qw), F32),
                        pltpu.VMEM((SUBLANES, vw), F32)],
        compiler_params=_cparams(("arbitrary", "arbitrary", "arbitrary")),
        name="gdn_prompt",
    )(qkv, qkv, qkv, conv_w, conv_w, conv_w, z, gates, norm_w.reshape(DEPTH, 1, HEAD_DIM), _inv_masks())


def _gdn_sample_prep_body(x_ref, c_ref, w_ref, o_ref, *, normalize):
    w = w_ref[...]
    y = c_ref[0] * w[0:1]
    for j in range(1, CONV_W - 1):
        y = y + c_ref[j] * w[j:j + 1]
    y = y + x_ref[...] * w[CONV_W - 1:CONV_W]
    y = _silu(y)
    if normalize:
        is_q = pl.program_id(0) < QK_W // y.shape[1]
        scale = jnp.where(is_q, HEAD_DIM ** -0.5, 1.0)
        for h in range(y.shape[1] // HEAD_DIM):
            yh = y[:, h * HEAD_DIM:(h + 1) * HEAD_DIM]
            yh = yh * lax.rsqrt(jnp.sum(yh * yh, axis=-1, keepdims=True) + EPS) * scale
            o_ref[:, h * HEAD_DIM:(h + 1) * HEAD_DIM] = yh
    else:
        o_ref[...] = y


def _gdn_sample_prep(qkv, cache_t, conv_w, *, layer, col0, n, normalize):
    tn = 1024
    cb0 = col0 // tn
    rb = N_PROMPT // DEC_BATCH
    return pl.pallas_call(
        functools.partial(_gdn_sample_prep_body, normalize=normalize),
        grid=(n // tn,),
        in_specs=[
            pl.BlockSpec((DEC_BATCH, tn), lambda j: (rb, cb0 + j)),
            pl.BlockSpec((None, CONV_W - 1, DEC_BATCH, tn), lambda j: (layer, 0, 0, cb0 + j)),
            pl.BlockSpec((None, CONV_W, tn), lambda j: (layer, 0, cb0 + j)),
        ],
        out_specs=pl.BlockSpec((DEC_BATCH, tn), lambda j: (0, j)),
        out_shape=jax.ShapeDtypeStruct((DEC_BATCH, n), F32),
        compiler_params=_cparams(("arbitrary",)),
        name="gdn_sample_prep",
    )(qkv, cache_t, conv_w)


def _gdn_sample_body(s_ref, qt_ref, kt_ref, v_ref, z_ref, eg_ref, beta_ref, nw_ref, so_ref, o_ref):
    qt = qt_ref[...]
    kt = kt_ref[...]
    v = v_ref[...]
    z = z_ref[...]
    egs = eg_ref[...]
    betas = beta_ref[...]
    nw = nw_ref[...]
    for h in range(V_HEADS):
        hq = h // 2
        kc = kt[:, hq:hq + 1]
        qc = qt[:, hq:hq + 1]
        s = s_ref[h]
        eg = egs[:, h:h + 1]
        beta = betas[:, h:h + 1]
        ks = jnp.sum(s * kc, axis=0, keepdims=True)
        qs = jnp.sum(s * qc, axis=0, keepdims=True)
        vh = v[:, h * HEAD_DIM:(h + 1) * HEAD_DIM]
        v_new = beta * vh - (beta * eg) * ks
        qk = jnp.sum(qc * kc, axis=0, keepdims=True)
        o = eg * qs + qk * v_new
        so_ref[h] = s * eg + kc * v_new
        zz = z[:, h * HEAD_DIM:(h + 1) * HEAD_DIM]
        on = o * lax.rsqrt(jnp.mean(o * o, axis=-1, keepdims=True) + EPS) * nw * _silu(zz)
        o_ref[:, h * HEAD_DIM:(h + 1) * HEAD_DIM] = on.astype(o_ref.dtype)


def _gdn_sample(state, qt, kt, v, z, eg, beta, norm_w, *, layer):
    r3 = lambda w: pl.BlockSpec((None, 1, w), lambda b: (b, 0, 0))
    return pl.pallas_call(
        _gdn_sample_body,
        grid=(DEC_BATCH,),
        in_specs=[
            pl.BlockSpec((None, None, V_HEADS, HEAD_DIM, HEAD_DIM), lambda b: (layer, b, 0, 0, 0)),
            pl.BlockSpec((None, HEAD_DIM, QK_HEADS), lambda b: (b, 0, 0)),
            pl.BlockSpec((None, HEAD_DIM, QK_HEADS), lambda b: (b, 0, 0)),
            r3(V_W), r3(V_W), r3(V_HEADS), r3(V_HEADS),
            pl.BlockSpec((None, 1, HEAD_DIM), lambda b: (layer, 0, 0)),
        ],
        out_specs=[
            pl.BlockSpec((None, V_HEADS, HEAD_DIM, HEAD_DIM), lambda b: (b, 0, 0, 0)),
            r3(V_W),
        ],
        out_shape=[jax.ShapeDtypeStruct((DEC_BATCH, V_HEADS, HEAD_DIM, HEAD_DIM), F32),
                   jax.ShapeDtypeStruct((DEC_BATCH, 1, V_W), BF16)],
        compiler_params=_cparams(("arbitrary",)),
        name="gdn_sample",
    )(state, qt, kt, v, z, eg, beta, norm_w.reshape(DEPTH, 1, HEAD_DIM))


def _s5_in_proj(ub, wb_ref, re_ref, im_ref):
    for j in range(S5_NBLK):
        r = jnp.dot(ub[:, j * LANES:(j + 1) * LANES], wb_ref[j], preferred_element_type=F32)
        re_ref[:, j * S5_BS:(j + 1) * S5_BS] = r[:, :S5_BS]
        im_ref[:, j * S5_BS:(j + 1) * S5_BS] = r[:, S5_BS:]


def _s5_out_proj(u, re_ref, im_ref, wc_ref, d_ref, y_ref):
    for j in range(S5_NBLK):
        xr = re_ref[:, j * S5_BS:(j + 1) * S5_BS].astype(BF16)
        xi = im_ref[:, j * S5_BS:(j + 1) * S5_BS].astype(BF16)
        y = jnp.dot(xr, wc_ref[j, :S5_BS], preferred_element_type=F32)
        y = y + jnp.dot(xi, wc_ref[j, S5_BS:], preferred_element_type=F32)
        y = y + d_ref[:, j * LANES:(j + 1) * LANES] * u[:, j * LANES:(j + 1) * LANES]
        y_ref[:, j * LANES:(j + 1) * LANES] = _gelu_tanh(y)


def _s5_prompt_body(*refs):
    u_refs = refs[:BATCH]
    wb_ref, wc_ref, lr_ref, li_ref, d_ref, y_ref, sre_ref, sim_ref, re_ref, im_ref, utm_ref, ytm_ref = refs[BATCH:]

    @pl.when(pl.program_id(0) == 0)
    def _():
        sre_ref[...] = jnp.zeros_like(sre_ref)
        sim_ref[...] = jnp.zeros_like(sim_ref)
        utm_ref[...] = jnp.zeros_like(utm_ref)

    for b in range(BATCH):
        for j in range(S5_NBLK):
            utm_ref[j, pl.ds(b, S5_TT, stride=SUBLANES), :] = u_refs[b][:, j * LANES:(j + 1) * LANES]
    for j in range(S5_NBLK):
        r = jnp.dot(utm_ref[j].astype(BF16), wb_ref[j], preferred_element_type=F32)
        re_ref[:, j * S5_BS:(j + 1) * S5_BS] = r[:, :S5_BS]
        im_ref[:, j * S5_BS:(j + 1) * S5_BS] = r[:, S5_BS:]
    cw = 1024
    for cb in range(S5_STATES // cw):
        cols = slice(cb * cw, (cb + 1) * cw)
        lr = lr_ref[:, cols]
        li = li_ref[:, cols]

        def step(t, carry):
            xr, xi = carry
            rows = pl.ds(pl.multiple_of(t * SUBLANES, SUBLANES), SUBLANES)
            nr = lr * xr - li * xi + re_ref[rows, cols]
            ni = lr * xi + li * xr + im_ref[rows, cols]
            re_ref[rows, cols] = nr
            im_ref[rows, cols] = ni
            return nr, ni

        xr, xi = lax.fori_loop(0, S5_TT, step, (sre_ref[:, cols], sim_ref[:, cols]), unroll=2)
        sre_ref[:, cols] = xr
        sim_ref[:, cols] = xi
    for j in range(S5_NBLK):
        xr = re_ref[:, j * S5_BS:(j + 1) * S5_BS].astype(BF16)
        xi = im_ref[:, j * S5_BS:(j + 1) * S5_BS].astype(BF16)
        y = jnp.dot(xr, wc_ref[j, :S5_BS], preferred_element_type=F32)
        y = y + jnp.dot(xi, wc_ref[j, S5_BS:], preferred_element_type=F32)
        y = y + d_ref[:, j * LANES:(j + 1) * LANES] * utm_ref[j]
        ytm_ref[j] = _gelu_tanh(y)
    for b in range(BATCH):
        for j in range(S5_NBLK):
            y_ref[b, :, j * LANES:(j + 1) * LANES] = ytm_ref[j, pl.ds(b, S5_TT, stride=SUBLANES), :]


def _s5_prompt(tail, wb, wc, lam_re8, lam_im8, d_row):
    rows = S5_TT * SUBLANES
    steps = SEQ // S5_TT
    full = lambda shape: pl.BlockSpec(shape, lambda t: (0,) * len(shape))
    return pl.pallas_call(
        _s5_prompt_body,
        grid=(steps,),
        in_specs=[pl.BlockSpec((S5_TT, D_MODEL), functools.partial(lambda b, t: (b * steps + t, 0), b))
                  for b in range(BATCH)] + [
            full((S5_NBLK, LANES, 2 * S5_BS)),
            full((S5_NBLK, 2 * S5_BS, LANES)),
            full((SUBLANES, S5_STATES)),
            full((SUBLANES, S5_STATES)),
            full((1, D_MODEL)),
        ],
        out_specs=[
            pl.BlockSpec((BATCH, S5_TT, D_MODEL), lambda t: (0, t, 0)),
            full((SUBLANES, S5_STATES)),
            full((SUBLANES, S5_STATES)),
        ],
        out_shape=[jax.ShapeDtypeStruct((BATCH, SEQ, D_MODEL), F32),
                   jax.ShapeDtypeStruct((SUBLANES, S5_STATES), F32),
                   jax.ShapeDtypeStruct((SUBLANES, S5_STATES), F32)],
        scratch_shapes=[pltpu.VMEM((rows, S5_STATES), F32), pltpu.VMEM((rows, S5_STATES), F32),
                        pltpu.VMEM((S5_NBLK, rows, LANES), F32), pltpu.VMEM((S5_NBLK, rows, LANES), F32)],
        compiler_params=_cparams(("arbitrary",)),
        name="s5_prompt",
    )(*([tail] * BATCH), wb, wc, lam_re8, lam_im8, d_row)


def _s5_sample_body(u_ref, x0r_ref, x0i_ref, wb_ref, wc_ref, lr_ref, li_ref, d_ref, y_ref, x1r_ref, x1i_ref):
    u = u_ref[...]
    _s5_in_proj(u.astype(BF16), wb_ref, x1r_ref, x1i_ref)
    lr = lr_ref[...]
    li = li_ref[...]
    xr = x0r_ref[...]
    xi = x0i_ref[...]
    x1r_ref[...] = x1r_ref[...] + (lr * xr - li * xi)
    x1i_ref[...] = x1i_ref[...] + (lr * xi + li * xr)
    _s5_out_proj(u, x1r_ref, x1i_ref, wc_ref, d_ref, y_ref)


def _s5_sample(u_all, x0r, x0i, wb, wc, lam_re, lam_im, d_row, *, layer):
    full = lambda shape: pl.BlockSpec(shape, lambda i: (0,) * len(shape))
    st = pl.BlockSpec((None, DEC_BATCH, S5_STATES), lambda i: (layer, 0, 0))
    return pl.pallas_call(
        _s5_sample_body,
        grid=(1,),
        in_specs=[
            pl.BlockSpec((DEC_BATCH, D_MODEL), lambda i: (N_PROMPT // DEC_BATCH, 0)),
            st, st,
            full((S5_NBLK, LANES, 2 * S5_BS)),
            full((S5_NBLK, 2 * S5_BS, LANES)),
            full((1, S5_STATES)), full((1, S5_STATES)), full((1, D_MODEL)),
        ],
        out_specs=[full((DEC_BATCH, D_MODEL)), full((DEC_BATCH, S5_STATES)), full((DEC_BATCH, S5_STATES))],
        out_shape=[jax.ShapeDtypeStruct((DEC_BATCH, D_MODEL), F32),
                   jax.ShapeDtypeStruct((DEC_BATCH, S5_STATES), F32),
                   jax.ShapeDtypeStruct((DEC_BATCH, S5_STATES), F32)],
        compiler_params=_cparams(("arbitrary",)),
        name="s5_sample",
    )(u_all, x0r, x0i, wb, wc, lam_re, lam_im, d_row)


def _s5_params(lam_re, lam_im, log_dt, b_re, b_im, c_re, c_im, d_skip):
    lam = lax.complex(lam_re, lam_im)
    dt = jnp.exp(log_dt)[:, None]
    lam_bar = jnp.exp(lam * dt)
    b_bar = ((lam_bar - 1.0) / lam)[..., None] * lax.complex(b_re, b_im)
    eye = jnp.eye(S5_BLK, dtype=F32)

    def in_blocks(b):
        b = b.reshape(S5_NBLK, S5_BLK, S5_N, S5_P)
        return jnp.einsum("jgnp,gh->jgphn", b, eye).reshape(S5_NBLK, S5_BLK * S5_P, S5_BS)

    def out_blocks(c):
        c = c.reshape(S5_NBLK, S5_BLK, S5_P, S5_N)
        return jnp.einsum("jgpn,gh->jgnhp", c, eye).reshape(S5_NBLK, S5_BS, S5_BLK * S5_P)

    wb = jnp.concatenate([in_blocks(b_bar.real), in_blocks(b_bar.imag)], axis=2).astype(BF16)
    wc = jnp.concatenate([out_blocks(c_re), -out_blocks(c_im)], axis=1).astype(BF16)
    return (wb, wc, lam_bar.real.reshape(1, S5_STATES), lam_bar.imag.reshape(1, S5_STATES),
            d_skip.reshape(1, D_MODEL))


def _router_body(h_ref, w_ref, b_ref, idx_ref, wt_ref):
    logits = jnp.dot(h_ref[...].astype(BF16), w_ref[...].astype(BF16), preferred_element_type=F32)
    scores = _sigmoid(logits)
    biased = scores + b_ref[...]
    lane = lax.broadcasted_iota(jnp.int32, biased.shape, 1)
    kcol = lax.broadcasted_iota(jnp.int32, (biased.shape[0], TOP_K), 1)
    idx = jnp.zeros((biased.shape[0], TOP_K), jnp.int32)
    sel = jnp.zeros((biased.shape[0], TOP_K), F32)
    for k in range(TOP_K):
        m = jnp.max(biased, axis=-1, keepdims=True)
        first = jnp.min(jnp.where(biased == m, lane, N_EXPERTS), axis=-1, keepdims=True)
        hit = lane == first
        s = jnp.sum(jnp.where(hit, scores, 0.0), axis=-1, keepdims=True)
        idx = jnp.where(kcol == k, first, idx)
        sel = jnp.where(kcol == k, s, sel)
        biased = jnp.where(hit, -jnp.inf, biased)
    idx_ref[...] = idx
    wt_ref[...] = sel / jnp.sum(sel, axis=-1, keepdims=True) * ROUTED_SCALE


def _router(h, w_router, router_bias, *, layer):
    tm = MM_TM
    return pl.pallas_call(
        _router_body,
        grid=(M_ROWS // tm,),
        in_specs=[
            pl.BlockSpec((tm, D_MODEL), lambda i: (i, 0)),
            pl.BlockSpec((None, D_MODEL, N_EXPERTS), lambda i: (layer, 0, 0)),
            pl.BlockSpec((None, 1, N_EXPERTS), lambda i: (layer, 0, 0)),
        ],
        out_specs=[pl.BlockSpec((tm, TOP_K), lambda i: (i, 0))] * 2,
        out_shape=[jax.ShapeDtypeStruct((M_ROWS, TOP_K), jnp.int32),
                   jax.ShapeDtypeStruct((M_ROWS, TOP_K), F32)],
        compiler_params=_cparams(("arbitrary",)),
        name="router",
    )(h, w_router, router_bias.reshape(DEPTH, 1, N_EXPERTS))


def _routing_tables(top_idx):
    flat_e = top_idx.reshape(-1)
    order = jnp.argsort(flat_e).astype(jnp.int32)
    sorted_e = flat_e[order]
    sizes = jnp.bincount(flat_e, length=N_EXPERTS).astype(jnp.int32)
    starts = jnp.cumsum(sizes) - sizes
    padded = (sizes + MOE_BS - 1) // MOE_BS * MOE_BS
    pad_ends = jnp.cumsum(padded)
    pad_starts = pad_ends - padded
    dest = (pad_starts[sorted_e] + jnp.arange(N_ASSIGN, dtype=jnp.int32) - starts[sorted_e]).astype(jnp.int32)
    _, pos = lax.sort((order, dest), num_keys=1)
    blk_start = jnp.arange(MOE_BLOCKS, dtype=jnp.int32) * MOE_BS
    block_e = jnp.minimum(jnp.sum(blk_start[:, None] >= pad_ends[None, :], axis=1), N_EXPERTS - 1).astype(jnp.int32)
    row_e = jnp.repeat(block_e, MOE_BS, total_repeat_length=MOE_ROWS)
    j = jnp.arange(MOE_ROWS, dtype=jnp.int32) - pad_starts[row_e]
    src = jnp.clip(starts[row_e] + j, 0, N_ASSIGN - 1)
    row_tok = jnp.where(j < sizes[row_e], order[src] // TOP_K, 0).astype(jnp.int32)
    return row_tok, pos, block_e


def _to_slabs(x):
    return x.reshape(x.shape[0] * ROW_SLAB, LANES)


def _load_slab_rows(ref, n):
    return jnp.concatenate([ref[pl.ds(j, n, stride=SLAB_PITCH), :] for j in range(ROW_SLAB)], axis=1)


def _expert_gather(tok_ref, h_hbm, xbuf, sem, slot):
    for r in range(MOE_BS):
        src = h_hbm.at[pl.ds(pl.multiple_of(tok_ref[0, r] * ROW_SLAB, ROW_SLAB), ROW_SLAB), :]
        pltpu.make_async_copy(src, xbuf.at[slot, pl.ds(r * SLAB_PITCH, ROW_SLAB), :], sem.at[slot]).start()


def _expert_gather_wait(h_hbm, xbuf, sem, slot):
    pltpu.make_async_copy(h_hbm.at[pl.ds(0, MOE_BS * ROW_SLAB), :], xbuf.at[slot, pl.ds(0, MOE_BS * ROW_SLAB), :],
                          sem.at[slot]).wait()


def _experts_body(be_ref, tok_ref, tokn_ref, h_hbm, wg_ref, wu_ref, wd_ref, o_ref,
                  xbuf, sem, wgb, wub, wdb):
    b = pl.program_id(0)
    nb = pl.num_programs(0)
    slot = lax.rem(b, 2)

    @pl.when(b == 0)
    def _():
        _expert_gather(tok_ref, h_hbm, xbuf, sem, 0)

    changed = jnp.logical_or(b == 0, be_ref[b] != be_ref[jnp.maximum(b - 1, 0)])

    @pl.when(changed)
    def _():
        wgb[...] = wg_ref[...].astype(BF16)
        wub[...] = wu_ref[...].astype(BF16)
        wdb[...] = wd_ref[...].astype(BF16)

    _expert_gather_wait(h_hbm, xbuf, sem, slot)
    _expert_gather(tokn_ref, h_hbm, xbuf, sem, 1 - slot)
    x = _load_slab_rows(xbuf.at[slot], MOE_BS).astype(BF16)
    g = jnp.dot(x, wgb[...], preferred_element_type=F32)
    u = jnp.dot(x, wub[...], preferred_element_type=F32)
    a = (_silu(g) * u).astype(BF16)
    y = jnp.dot(a, wdb[...], preferred_element_type=F32)
    for j in range(ROW_SLAB):
        o_ref[pl.ds(j, MOE_BS, stride=ROW_SLAB), :] = y[:, j * LANES:(j + 1) * LANES]

    @pl.when(b == nb - 1)
    def _():
        _expert_gather_wait(h_hbm, xbuf, sem, 1 - slot)


def _experts(h_slabs, row_tok, block_e, w_gate, w_up, w_down, *, layer):
    tok3 = row_tok.reshape(MOE_BLOCKS, 1, MOE_BS)
    smem_tok = lambda f: pl.BlockSpec((None, 1, MOE_BS), f, memory_space=pltpu.SMEM)
    grid_spec = pltpu.PrefetchScalarGridSpec(
        num_scalar_prefetch=1,
        grid=(MOE_BLOCKS,),
        in_specs=[
            smem_tok(lambda b, be: (b, 0, 0)),
            smem_tok(lambda b, be: (jnp.minimum(b + 1, MOE_BLOCKS - 1), 0, 0)),
            pl.BlockSpec(memory_space=pl.ANY),
            pl.BlockSpec((None, None, D_MODEL, D_EXPERT), lambda b, be: (layer, be[b], 0, 0)),
            pl.BlockSpec((None, None, D_MODEL, D_EXPERT), lambda b, be: (layer, be[b], 0, 0)),
            pl.BlockSpec((None, None, D_EXPERT, D_MODEL), lambda b, be: (layer, be[b], 0, 0)),
        ],
        out_specs=pl.BlockSpec((MOE_BS * ROW_SLAB, LANES), lambda b, be: (b, 0)),
        scratch_shapes=[
            pltpu.VMEM((2, MOE_BS * SLAB_PITCH, LANES), F32),
            pltpu.SemaphoreType.DMA((2,)),
            pltpu.VMEM((D_MODEL, D_EXPERT), BF16),
            pltpu.VMEM((D_MODEL, D_EXPERT), BF16),
            pltpu.VMEM((D_EXPERT, D_MODEL), BF16),
        ],
    )
    return pl.pallas_call(
        _experts_body,
        grid_spec=grid_spec,
        out_shape=jax.ShapeDtypeStruct((MOE_ROWS * ROW_SLAB, LANES), F32),
        compiler_params=_cparams(("arbitrary",)),
        name="experts",
    )(block_e, tok3, tok3, h_slabs, w_gate, w_up, w_down)


def _shared_body(h_ref, wg_ref, wu_ref, wd_ref, o_ref, wgb, wub, wdb):
    @pl.when(pl.program_id(0) == 0)
    def _():
        wgb[...] = wg_ref[...].astype(BF16)
        wub[...] = wu_ref[...].astype(BF16)
        wdb[...] = wd_ref[...].astype(BF16)

    x = h_ref[...].astype(BF16)
    g = jnp.dot(x, wgb[...], preferred_element_type=F32)
    u = jnp.dot(x, wub[...], preferred_element_type=F32)
    a = (_silu(g) * u).astype(BF16)
    o_ref[...] = jnp.dot(a, wdb[...], preferred_element_type=F32)


def _shared_expert(h, w_gate, w_up, w_down, *, layer):
    tm = MM_TM
    return pl.pallas_call(
        _shared_body,
        grid=(M_ROWS // tm,),
        in_specs=[
            pl.BlockSpec((tm, D_MODEL), lambda i: (i, 0)),
            pl.BlockSpec((None, D_MODEL, D_EXPERT), lambda i: (layer, 0, 0)),
            pl.BlockSpec((None, D_MODEL, D_EXPERT), lambda i: (layer, 0, 0)),
            pl.BlockSpec((None, D_EXPERT, D_MODEL), lambda i: (layer, 0, 0)),
        ],
        out_specs=pl.BlockSpec((tm, D_MODEL), lambda i: (i, 0)),
        out_shape=jax.ShapeDtypeStruct((M_ROWS, D_MODEL), F32),
        scratch_shapes=[pltpu.VMEM((D_MODEL, D_EXPERT), BF16), pltpu.VMEM((D_MODEL, D_EXPERT), BF16),
                        pltpu.VMEM((D_EXPERT, D_MODEL), BF16)],
        compiler_params=_cparams(("arbitrary",)),
        name="shared_expert",
    )(h, w_gate, w_up, w_down)


def _combine_gather(pos_ref, y_hbm, buf, sem, slot):
    def issue(t, carry):
        for k in range(TOP_K):
            p = pos_ref[0, t * TOP_K + k]
            src = y_hbm.at[pl.ds(pl.multiple_of(p * ROW_SLAB, ROW_SLAB), ROW_SLAB), :]
            dst = buf.at[slot, k, pl.ds(t * SLAB_PITCH, ROW_SLAB), :]
            pltpu.make_async_copy(src, dst, sem.at[slot]).start()
        return carry

    lax.fori_loop(0, COMB_T, issue, 0, unroll=2)


def _combine_gather_wait(y_hbm, buf, sem, slot):
    for k in range(TOP_K):
        pltpu.make_async_copy(y_hbm.at[pl.ds(0, COMB_T * ROW_SLAB), :],
                              buf.at[slot, k, pl.ds(0, COMB_T * ROW_SLAB), :], sem.at[slot]).wait()


def _combine_body(pos_ref, posn_ref, y_hbm, wt_ref, sh_ref, o_ref, buf, sem):
    i = pl.program_id(0)
    slot = lax.rem(i, 2)

    @pl.when(i == 0)
    def _():
        _combine_gather(pos_ref, y_hbm, buf, sem, 0)

    @pl.when(i + 1 < pl.num_programs(0))
    def _():
        _combine_gather(posn_ref, y_hbm, buf, sem, 1 - slot)

    _combine_gather_wait(y_hbm, buf, sem, slot)
    wt = wt_ref[...]
    wk = [jnp.broadcast_to(wt[:, k:k + 1], (COMB_T, LANES)) for k in range(TOP_K)]
    for j in range(ROW_SLAB):
        cols = slice(j * LANES, (j + 1) * LANES)
        acc = sh_ref[:, cols]
        for k in range(TOP_K):
            acc = acc + buf[slot, k, pl.ds(j, COMB_T, stride=SLAB_PITCH), :] * wk[k]
        o_ref[:, cols] = acc


def _combine(y_slabs, pos, top_w, shared):
    nt = M_ROWS // COMB_T
    pos3 = pos.reshape(nt, 1, COMB_T * TOP_K)
    smem_pos = lambda f: pl.BlockSpec((None, 1, COMB_T * TOP_K), f, memory_space=pltpu.SMEM)
    return pl.pallas_call(
        _combine_body,
        grid=(nt,),
        in_specs=[
            smem_pos(lambda i: (i, 0, 0)),
            smem_pos(lambda i: (jnp.minimum(i + 1, nt - 1), 0, 0)),
            pl.BlockSpec(memory_space=pl.ANY),
            pl.BlockSpec((COMB_T, TOP_K), lambda i: (i, 0)),
            pl.BlockSpec((COMB_T, D_MODEL), lambda i: (i, 0)),
        ],
        out_specs=pl.BlockSpec((COMB_T, D_MODEL), lambda i: (i, 0)),
        out_shape=jax.ShapeDtypeStruct((M_ROWS, D_MODEL), F32),
        scratch_shapes=[pltpu.VMEM((2, TOP_K, COMB_T * SLAB_PITCH, LANES), F32), pltpu.SemaphoreType.DMA((2,))],
        compiler_params=_cparams(("arbitrary",)),
        name="combine",
    )(pos3, pos3, y_slabs, top_w, shared)


def _mixer(l, h1, p, cache_t, state_gdn, s5_x0r, s5_x0i):
    w_in = p["w_in"]
    qkv = _mm(h1, w_in, layer=l, col0=0, n=QKV_W, tn=1024, name="in_qkv")
    z = _mm(h1, w_in, layer=l, col0=COL_Z, n=V_W, tn=1024, name="in_z")
    ba = _mm(h1, p["w_ba"], layer=l, col0=0, n=LANES, tn=LANES, name="in_ba")
    tail = _mm(h1, p["w_tail"], layer=l, col0=0, n=3 * D_MODEL, tn=1024, name="in_tail")

    gb, gcum = _gates(ba, p["alog_row"][l], p["dtb_row"][l])
    o_p, gdn_p = _gdn_prompt(qkv, z, p["conv_w"], _gdn_gate_rows(gb, gcum), p["gdn_norm_w"], layer=l)

    qk_s = _gdn_sample_prep(qkv, cache_t, p["conv_w"], layer=l, col0=0, n=2 * QK_W, normalize=True)
    v_s = _gdn_sample_prep(qkv, cache_t, p["conv_w"], layer=l, col0=2 * QK_W, n=V_W, normalize=False)
    col_form = lambda x: x.reshape(DEC_BATCH, QK_HEADS, HEAD_DIM).transpose(0, 2, 1)
    qt = col_form(qk_s[:, :QK_W])
    kt = col_form(qk_s[:, QK_W:])
    beta_s = gb[N_PROMPT:, :V_HEADS].reshape(DEC_BATCH, 1, V_HEADS)
    eg_s = jnp.exp(gb[N_PROMPT:, V_HEADS:2 * V_HEADS]).reshape(DEC_BATCH, 1, V_HEADS)
    z_s = z[N_PROMPT:].reshape(DEC_BATCH, 1, V_W)
    gdn_s, o_s = _gdn_sample(state_gdn, qt, kt, v_s.reshape(DEC_BATCH, 1, V_W), z_s, eg_s, beta_s,
                             p["gdn_norm_w"], layer=l)
    o_all = jnp.concatenate([o_p, o_s.reshape(DEC_BATCH, V_W)], axis=0)
    y_a = _mm(o_all, p["w_br_a"], layer=l, col0=0, n=D_MODEL, tn=512, name="br_a")

    wb, wc, lam_re, lam_im, d_row = p["s5"][l]
    lam_re8 = jnp.broadcast_to(lam_re, (SUBLANES, S5_STATES))
    lam_im8 = jnp.broadcast_to(lam_im, (SUBLANES, S5_STATES))
    y_p, re_p, im_p = _s5_prompt(tail, wb, wc, lam_re8, lam_im8, d_row)
    y_s, re_s, im_s = _s5_sample(tail, s5_x0r, s5_x0i, wb, wc, lam_re, lam_im, d_row, layer=l)
    y_all = jnp.concatenate([y_p.reshape(N_PROMPT, D_MODEL), y_s], axis=0)
    ys_spec = pl.BlockSpec((MM_TM, 1024), lambda j, i: (i, j))
    y_glu = _mm(y_all, p["w_glu"], layer=l, col0=0, n=D_MODEL, tn=1024, out_dtype=BF16, epilogue="glu",
                extra=(y_all,), extra_specs=(ys_spec,), name="glu")

    tn = 1024
    nb = D_MODEL // tn
    merged = _mm(y_glu, p["w_br_b"], layer=l, col0=0, n=D_MODEL, tn=tn, out_dtype=BF16, epilogue="merge",
                 extra=(y_a, tail, tail),
                 extra_specs=(pl.BlockSpec((MM_TM, tn), lambda j, i: (i, j)),
                              pl.BlockSpec((MM_TM, tn), lambda j, i: (i, nb + j)),
                              pl.BlockSpec((MM_TM, tn), lambda j, i: (i, 2 * nb + j))),
                 name="br_b_merge")
    mix = _mm(merged, p["w_out"], layer=l, col0=0, n=D_MODEL, tn=1024, name="out_proj")

    conv_p = jnp.stack([qkv[(b + 1) * SEQ - (CONV_W - 1):(b + 1) * SEQ] for b in range(BATCH)])
    conv_s = jnp.concatenate([cache_t[l, 1:].transpose(1, 0, 2), qkv[N_PROMPT:, None, :]], axis=1)
    states = (conv_p, gdn_p,
              re_p[:BATCH].reshape(BATCH, S5_G, S5_N), im_p[:BATCH].reshape(BATCH, S5_G, S5_N),
              conv_s, gdn_s,
              re_s.reshape(DEC_BATCH, S5_G, S5_N), im_s.reshape(DEC_BATCH, S5_G, S5_N))
    return mix, states


def _moe(l, h2, p):
    top_idx, top_w = _router(h2, p["w_router"], p["router_bias"], layer=l)
    row_tok, pos, block_e = _routing_tables(top_idx)
    y_slabs = _experts(_to_slabs(h2), row_tok, block_e, p["w_exp_gate"], p["w_exp_up"], p["w_exp_down"], layer=l)
    shared = _shared_expert(h2, p["w_sh_gate"], p["w_sh_up"], p["w_sh_down"], layer=l)
    return _combine(y_slabs, pos, top_w, shared)


def kernel(x_prompt, x_sample, cache_conv, state_gdn, state_s5_re, state_s5_im, c_prompt, c_sample, ln_in_g, ln_in_b, w_ada, b_ada, w_in, conv_w, gdn_a_log, gdn_dt_bias, gdn_norm_w, s5_lam_re, s5_lam_im, s5_log_dt, s5_b_re, s5_b_im, s5_c_re, s5_c_im, s5_d, w_glu, w_br_a, w_br_b, w_out, ln1_g, ln1_b, w_router, router_bias, w_exp_gate, w_exp_up, w_exp_down, w_sh_gate, w_sh_up, w_sh_down, ln2_g, ln2_b):
    c_all = jnp.concatenate([c_prompt, jnp.zeros((SUBLANES - BATCH, D_MODEL), F32), c_sample], axis=0)
    mod = _ada_all(c_all, w_ada, b_ada)
    mod_p = mod[:, :SUBLANES]
    mod_s = mod[:, SUBLANES:]

    lane_pad = lambda a: jnp.pad(a, ((0, 0), (V_HEADS, LANES - 2 * V_HEADS)))[:, None, :]
    p = {
        "w_in": w_in,
        "w_ba": jnp.pad(w_in[:, :, COL_BA:COL_TAIL], ((0, 0), (0, 0), (0, LANES - 2 * V_HEADS))),
        "w_tail": w_in[:, :, COL_TAIL:],
        "alog_row": lane_pad(gdn_a_log), "dtb_row": lane_pad(gdn_dt_bias),
        "conv_w": conv_w, "gdn_norm_w": gdn_norm_w,
        "s5": [_s5_params(s5_lam_re[l], s5_lam_im[l], s5_log_dt[l], s5_b_re[l], s5_b_im[l],
                          s5_c_re[l], s5_c_im[l], s5_d[l]) for l in range(DEPTH)],
        "w_glu": w_glu, "w_br_a": w_br_a, "w_br_b": w_br_b, "w_out": w_out,
        "w_router": w_router, "router_bias": router_bias,
        "w_exp_gate": w_exp_gate, "w_exp_up": w_exp_up, "w_exp_down": w_exp_down,
        "w_sh_gate": w_sh_gate, "w_sh_up": w_sh_up, "w_sh_down": w_sh_down,
    }
    cache_t = cache_conv.transpose(0, 2, 1, 3)
    s5_x0r = state_s5_re.reshape(DEPTH, DEC_BATCH, S5_STATES)
    s5_x0i = state_s5_im.reshape(DEPTH, DEC_BATCH, S5_STATES)

    x_all = jnp.concatenate([x_prompt.reshape(N_PROMPT, D_MODEL), x_sample.reshape(DEC_BATCH, D_MODEL)], axis=0)
    x, h1 = _ln_in(x_all, ln_in_g, ln_in_b, mod_p, mod_s)

    per_layer = []
    for l in range(DEPTH):
        mix, states = _mixer(l, h1, p, cache_t, state_gdn, s5_x0r, s5_x0i)
        per_layer.append(states)
        x, h2 = _deepnorm(x, mix, ln1_g, ln1_b, mod_p, mod_s, layer=l, gate_comp=2, mod_layer=l,
                          sc_comp=4, sh_comp=3, h_dtype=F32)
        ffn = _moe(l, h2, p)
        nxt = min(l + 1, DEPTH - 1)
        x, h1 = _deepnorm(x, ffn, ln2_g, ln2_b, mod_p, mod_s, layer=l, gate_comp=5, mod_layer=nxt,
                          sc_comp=1, sh_comp=0, h_dtype=BF16)

    stacked = [jnp.stack([s[i] for s in per_layer]) for i in range(8)]
    return (x[:N_PROMPT].reshape(BATCH, SEQ, D_MODEL), x[N_PROMPT:].reshape(DEC_BATCH, 1, D_MODEL), *stacked)
```

```python
import functools
import math

import jax
import jax.numpy as jnp
from jax import lax
from jax.experimental import pallas as pl
from jax.experimental.pallas import tpu as pltpu

F32 = jnp.float32
BF16 = jnp.bfloat16

D_MODEL = 2048
BATCH = 4
SEQ = 2048
DEPTH = 4
DEC_BATCH = 128
N_PROMPT = BATCH * SEQ
M_ROWS = N_PROMPT + DEC_BATCH

QK_HEADS = 16
V_HEADS = 32
HEAD_DIM = 128
QK_W = QK_HEADS * HEAD_DIM
V_W = V_HEADS * HEAD_DIM
QKV_W = 2 * QK_W + V_W
CONV_W = 4
CHUNK = 64
N_CHUNKS = SEQ // CHUNK

S5_P = 16
S5_N = 64
S5_G = D_MODEL // S5_P
S5_STATES = S5_G * S5_N

N_EXPERTS = 64
TOP_K = 8
D_EXPERT = D_MODEL // 4
ROUTED_SCALE = 2.5
N_ASSIGN = M_ROWS * TOP_K

ALPHA = (2 * DEPTH) ** 0.25
EPS = 1e-6

COL_Z = QKV_W
COL_BA = QKV_W + V_W
COL_TAIL = COL_BA + 2 * V_HEADS

LANES = 128
SUBLANES = 8
VMEM_LIMIT = 56 * 1024 * 1024

ROW_TILE = 128
N_ROW_TILES = M_ROWS // ROW_TILE
PROMPT_ROW_TILES = N_PROMPT // ROW_TILE
TILES_PER_SEQ = SEQ // ROW_TILE
MM_TM = 640

QUAD = 4 * CHUNK
GDN_QUADS = 4
GDN_PAIRS = 2 * GDN_QUADS
GDN_GROUPS = QK_HEADS // GDN_PAIRS
GDN_VH = 2 * GDN_PAIRS

S5_TT = 32
S5_BLK = 8
S5_NBLK = S5_G // S5_BLK
S5_BS = S5_BLK * S5_N

ROW_SLAB = D_MODEL // LANES
SLAB_PITCH = ROW_SLAB + 4
MOE_BS = 256
MOE_BLOCKS = -(-N_ASSIGN // MOE_BS) + N_EXPERTS
MOE_ROWS = MOE_BLOCKS * MOE_BS
COMB_T = 128


def _cparams(sem, vmem=VMEM_LIMIT):
    return pltpu.CompilerParams(dimension_semantics=sem, vmem_limit_bytes=vmem)


def _sigmoid(x):
    return jax.nn.sigmoid(x)


def _silu(x):
    return x * jax.nn.sigmoid(x)


def _gelu_tanh(x):
    c = math.sqrt(2.0 / math.pi)
    return 0.5 * x * (1.0 + jnp.tanh(c * (x + 0.044715 * (x * x * x))))


def _layer_norm(x, g, b):
    mu = jnp.mean(x, axis=-1, keepdims=True)
    xc = x - mu
    var = jnp.mean(xc * xc, axis=-1, keepdims=True)
    return xc * lax.rsqrt(var + EPS) * g + b


def _mm_body(*refs, prologue, epilogue, n_extra):
    a_ref, w_ref = refs[0], refs[1]
    extra = refs[2:2 + n_extra]
    o_ref = refs[2 + n_extra]
    wbf_ref = refs[3 + n_extra]

    @pl.when(pl.program_id(1) == 0)
    def _():
        wbf_ref[...] = w_ref[...].astype(BF16)

    a = a_ref[...]
    if prologue == "silu":
        a = _silu(a.astype(F32))
    acc = jnp.dot(a.astype(BF16), wbf_ref[...], preferred_element_type=F32)
    if epilogue == "bias":
        acc = acc + extra[0][...]
    elif epilogue == "glu":
        acc = extra[0][...].astype(F32) * _sigmoid(acc)
    elif epilogue == "merge":
        ya, ga, gb = extra[0][...], extra[1][...], extra[2][...]
        acc = _sigmoid(ga) * ya + _sigmoid(gb) * acc
    o_ref[...] = acc.astype(o_ref.dtype)


def _mm(a, w, *, layer, col0, n, tn, tm=MM_TM, out_dtype=F32, prologue=None, epilogue=None,
        extra=(), extra_specs=(), rows=None, name="mm"):
    rows = a.shape[0] if rows is None else rows
    k = a.shape[1]
    assert rows % tm == 0 and n % tn == 0 and col0 % tn == 0
    cb0 = col0 // tn
    grid = (n // tn, rows // tm)
    in_specs = [
        pl.BlockSpec((tm, k), lambda j, i: (i, 0)),
        pl.BlockSpec((None, k, tn), lambda j, i: (layer, 0, cb0 + j)),
    ] + list(extra_specs)
    return pl.pallas_call(
        functools.partial(_mm_body, prologue=prologue, epilogue=epilogue, n_extra=len(extra)),
        grid=grid,
        in_specs=in_specs,
        out_specs=pl.BlockSpec((tm, tn), lambda j, i: (i, j)),
        out_shape=jax.ShapeDtypeStruct((rows, n), out_dtype),
        scratch_shapes=[pltpu.VMEM((k, tn), BF16)],
        compiler_params=_cparams(("arbitrary", "arbitrary")),
        name=name,
    )(a, w, *extra)


def _ada_body(c_ref, w_ref, b_ref, o_ref):
    a = _silu(c_ref[...]).astype(BF16)
    o_ref[...] = jnp.dot(a, w_ref[...].astype(BF16), preferred_element_type=F32) + b_ref[...]


def _ada_all(c_all, w_ada, b_ada):
    rows = c_all.shape[0]
    tn = 1024
    nj = (6 * D_MODEL) // tn
    return pl.pallas_call(
        _ada_body,
        grid=(DEPTH * nj,),
        in_specs=[
            pl.BlockSpec((rows, D_MODEL), lambda g: (0, 0)),
            pl.BlockSpec((None, D_MODEL, tn), lambda g: (g // nj, 0, g % nj)),
            pl.BlockSpec((None, 1, tn), lambda g: (g // nj, 0, g % nj)),
        ],
        out_specs=pl.BlockSpec((None, rows, tn), lambda g: (g // nj, 0, g % nj)),
        out_shape=jax.ShapeDtypeStruct((DEPTH, rows, 6 * D_MODEL), F32),
        compiler_params=_cparams(("arbitrary",)),
        name="ada",
    )(c_all, w_ada, b_ada.reshape(DEPTH, 1, 6 * D_MODEL))


def _mod_pick(i, mp_ref, ms_ref):
    b = jnp.minimum(i // TILES_PER_SEQ, BATCH - 1)
    row = mp_ref[pl.ds(b, 1), :]
    return jnp.where(i < PROMPT_ROW_TILES, row, ms_ref[...])


def _mod_specs(layer, comp):
    return [
        pl.BlockSpec((None, SUBLANES, D_MODEL), lambda i: (layer, 0, comp)),
        pl.BlockSpec((None, DEC_BATCH, D_MODEL), lambda i: (layer, 0, comp)),
    ]


def _row_spec():
    return pl.BlockSpec((ROW_TILE, D_MODEL), lambda i: (i, 0))


def _vec_spec():
    return pl.BlockSpec((1, D_MODEL), lambda i: (0, 0))


def _ln_in_body(x_ref, g_ref, b_ref, scp, scs, shp, shs, x_out, h_out):
    i = pl.program_id(0)
    x = _layer_norm(x_ref[...], g_ref[...], b_ref[...])
    x_out[...] = x
    h = x * (1.0 + _mod_pick(i, scp, scs)) + _mod_pick(i, shp, shs)
    h_out[...] = h.astype(h_out.dtype)


def _ln_in(x_all, g, b, mod_p, mod_s):
    return pl.pallas_call(
        _ln_in_body,
        grid=(N_ROW_TILES,),
        in_specs=[_row_spec(), _vec_spec(), _vec_spec()] + _mod_specs(0, 1) + _mod_specs(0, 0),
        out_specs=[_row_spec(), _row_spec()],
        out_shape=[jax.ShapeDtypeStruct((M_ROWS, D_MODEL), F32),
                   jax.ShapeDtypeStruct((M_ROWS, D_MODEL), BF16)],
        compiler_params=_cparams(("arbitrary",)),
        name="ln_in",
    )(x_all, g.reshape(1, -1), b.reshape(1, -1), mod_p, mod_s, mod_p, mod_s)


def _deepnorm_body(x_ref, y_ref, g_ref, b_ref, gtp, gts, scp, scs, shp, shs, x_out, h_out):
    i = pl.program_id(0)
    r = ALPHA * x_ref[...] + _mod_pick(i, gtp, gts) * y_ref[...]
    x = _layer_norm(r, g_ref[...], b_ref[...])
    x_out[...] = x
    h = x * (1.0 + _mod_pick(i, scp, scs)) + _mod_pick(i, shp, shs)
    h_out[...] = h.astype(h_out.dtype)


def _deepnorm(x, y, g, b, mod_p, mod_s, *, layer, gate_comp, mod_layer, sc_comp, sh_comp, h_dtype):
    return pl.pallas_call(
        _deepnorm_body,
        grid=(N_ROW_TILES,),
        in_specs=[_row_spec(), _row_spec(),
                  pl.BlockSpec((None, 1, D_MODEL), lambda i: (layer, 0, 0)),
                  pl.BlockSpec((None, 1, D_MODEL), lambda i: (layer, 0, 0))]
        + _mod_specs(layer, gate_comp) + _mod_specs(mod_layer, sc_comp) + _mod_specs(mod_layer, sh_comp),
        out_specs=[_row_spec(), _row_spec()],
        out_shape=[jax.ShapeDtypeStruct((M_ROWS, D_MODEL), F32),
                   jax.ShapeDtypeStruct((M_ROWS, D_MODEL), h_dtype)],
        compiler_params=_cparams(("arbitrary",)),
        name="deepnorm",
    )(x, y, g.reshape(DEPTH, 1, D_MODEL), b.reshape(DEPTH, 1, D_MODEL),
      mod_p, mod_s, mod_p, mod_s, mod_p, mod_s)


def _gates_body(ba_ref, alog_ref, dtb_ref, gb_ref, gc_ref):
    x = ba_ref[...]
    lane = lax.broadcasted_iota(jnp.int32, x.shape, 1)
    beta = _sigmoid(x)
    z = x + dtb_ref[...]
    softplus = jnp.maximum(z, 0.0) + jnp.log1p(jnp.exp(-jnp.abs(z)))
    g = -jnp.exp(alog_ref[...]) * softplus
    is_beta = lane < V_HEADS
    is_g = jnp.logical_and(lane >= V_HEADS, lane < 2 * V_HEADS)
    g = jnp.where(is_g, g, 0.0)
    gb_ref[...] = jnp.where(is_beta, beta, g)
    r = lax.broadcasted_iota(jnp.int32, (ROW_TILE, ROW_TILE), 0)
    c = lax.broadcasted_iota(jnp.int32, (ROW_TILE, ROW_TILE), 1)
    tri = jnp.logical_and(c <= r, (r // CHUNK) == (c // CHUNK)).astype(F32)
    gc_ref[...] = jnp.dot(tri, g, preferred_element_type=F32, precision=lax.Precision.HIGHEST)


def _gates(ba, alog_row, dtb_row):
    spec = pl.BlockSpec((ROW_TILE, LANES), lambda i: (i, 0))
    vec = pl.BlockSpec((1, LANES), lambda i: (0, 0))
    return pl.pallas_call(
        _gates_body,
        grid=(N_ROW_TILES,),
        in_specs=[spec, vec, vec],
        out_specs=[spec, spec],
        out_shape=[jax.ShapeDtypeStruct((M_ROWS, LANES), F32)] * 2,
        compiler_params=_cparams(("arbitrary",)),
        name="gates",
    )(ba, alog_row, dtb_row)


def _dotb(x, y):
    return jnp.dot(x.astype(BF16), y.astype(BF16), preferred_element_type=F32)


def _inv_masks():
    r = jnp.arange(QUAD)[:, None]
    c = jnp.arange(QUAD)[None, :]
    same = lambda sh: (r >> sh) == (c >> sh)
    masks = [same(3)] + [jnp.logical_and(same(sh), jnp.logical_not(same(sh - 1))) for sh in (4, 5, 6)] + [r == c]
    return jnp.stack(masks).astype(BF16)


def _inv_unit_lower(mats, m_ref):
    ds = [a * m_ref[0] for a in mats]
    d2s = [_dotb(d, d).astype(BF16) for d in ds]
    d4s = [_dotb(d2, d2).astype(BF16) for d2 in d2s]
    ts = [m_ref[4].astype(F32) - d.astype(F32) for d in ds]
    ts = [t + _dotb(t, d2) for t, d2 in zip(ts, d2s)]
    ts = [t + _dotb(t, d4) for t, d4 in zip(ts, d4s)]
    for lvl in (1, 2, 3):
        tbs = [t.astype(BF16) for t in ts]
        ets = [_dotb(a * m_ref[lvl], tb).astype(BF16) for a, tb in zip(mats, tbs)]
        ts = [t - _dotb(tb, et) for t, tb, et in zip(ts, tbs, ets)]
    return ts


def _conv_silu(x_ref, hist_ref, w_ref):
    x = x_ref[...]
    w = w_ref[...]
    xc = jnp.concatenate([hist_ref[...], x], axis=0)
    base = SUBLANES - (CONV_W - 1)
    y = xc[base:base + CHUNK] * w[0:1]
    for j in range(1, CONV_W - 1):
        y = y + xc[base + j:base + j + CHUNK] * w[j:j + 1]
    y = y + x * w[CONV_W - 1:CONV_W]
    hist_ref[...] = x[CHUNK - SUBLANES:CHUNK]
    return _silu(y)


def _gdn_prompt_body(q_ref, k_ref, v_ref, wq_ref, wk_ref, wv_ref, z_ref, gates_ref, nw_ref, m_ref, o_ref, s_ref,
                     hq_ref, hk_ref, hv_ref):
    c = pl.program_id(2)

    @pl.when(c == 0)
    def _():
        s_ref[...] = jnp.zeros_like(s_ref)
        hq_ref[...] = jnp.zeros_like(hq_ref)
        hk_ref[...] = jnp.zeros_like(hk_ref)
        hv_ref[...] = jnp.zeros_like(hv_ref)

    s_old = [s_ref[j] for j in range(GDN_VH)]
    qa = _conv_silu(q_ref, hq_ref, wq_ref)
    ka = _conv_silu(k_ref, hk_ref, wk_ref)
    va = _conv_silu(v_ref, hv_ref, wv_ref)
    z = z_ref[...]
    nw = nw_ref[...]
    head = lambda x, j: x[:, j * HEAD_DIM:(j + 1) * HEAD_DIM]

    pr = 2 * CHUNK
    row = lax.broadcasted_iota(jnp.int32, (pr, pr), 0)
    col = lax.broadcasted_iota(jnp.int32, (pr, pr), 1)
    same_head = (row >> 6) == (col >> 6)
    causal = jnp.logical_and(same_head, row >= col)
    strict = jnp.logical_and(same_head, row > col)
    zero_b = jnp.zeros((pr, pr), BF16)

    blockdiag = lambda x: jnp.concatenate([jnp.concatenate([x[0], zero_b], axis=1),
                                           jnp.concatenate([zero_b, x[1]], axis=1)], axis=0)
    quads = range(GDN_QUADS)
    hrows = lambda x, h: x[h * CHUNK:(h + 1) * CHUNK]

    gcol, bcol, glcol, eg, kst, qst, vst, kts, a_mat, qkd = [], [], [], [], [], [], [], [], [], []
    for qd in quads:
        gates = gates_ref[qd]
        gcols = gates.T
        gcol.append(gcols[:, 0:1])
        bcol.append(gcols[:, 1:2])
        glcol.append(gcols[:, 2:3])
        grow = gates[0:1, :]
        eg.append(jnp.exp(gcol[qd]))
        kps, qps, kpts, a_blk, qkd_blk = [], [], [], [], []
        for pp in range(2):
            q = head(qa, 2 * qd + pp)
            k = head(ka, 2 * qd + pp)
            qn = q * lax.rsqrt(jnp.sum(q * q, axis=-1, keepdims=True) + EPS) * (HEAD_DIM ** -0.5)
            kn = k * lax.rsqrt(jnp.sum(k * k, axis=-1, keepdims=True) + EPS)
            kp = jnp.concatenate([kn, kn], axis=0)
            qp = jnp.concatenate([qn, qn], axis=0)
            kp_t = kp.T.astype(BF16)
            rs = slice(pp * pr, (pp + 1) * pr)
            dec = jnp.exp(jnp.where(causal, gcol[qd][rs] - grow[:, rs], -jnp.inf))
            a_blk.append(jnp.where(strict, bcol[qd][rs] * _dotb(kp, kp_t) * dec, 0.0).astype(BF16))
            qkd_blk.append((_dotb(qp, kp_t) * dec).astype(BF16))
            kps.append(kp)
            qps.append(qp)
            kpts.append(kp_t)
        kst.append(jnp.concatenate(kps, axis=0))
        qst.append(jnp.concatenate(qps, axis=0))
        vst.append(jnp.concatenate([head(va, 4 * qd + h) for h in range(4)], axis=0))
        kts.append(kpts)
        a_mat.append(blockdiag(a_blk))
        qkd.append(blockdiag(qkd_blk))

    t = _inv_unit_lower(a_mat, m_ref)
    sol = [_dotb(t[qd], jnp.concatenate([vst[qd] * bcol[qd], kst[qd] * (bcol[qd] * eg[qd])], axis=1)) for qd in quads]
    qdec = [qst[qd] * eg[qd] for qd in quads]
    res = [[_dotb(jnp.concatenate([hrows(sol[qd][:, HEAD_DIM:], h), hrows(qdec[qd], h)], axis=0), s_old[4 * qd + h])
            for qd in quads] for h in range(4)]
    v_new = [jnp.concatenate([hrows(sol[qd][:, :HEAD_DIM], h) - res[h][qd][:CHUNK] for h in range(4)], axis=0)
             for qd in quads]
    o = [jnp.concatenate([res[h][qd][CHUNK:] for h in range(4)], axis=0) + _dotb(qkd[qd], v_new[qd]) for qd in quads]
    v_sc = [v_new[qd] * jnp.exp(glcol[qd] - gcol[qd]) for qd in quads]
    upd = [[_dotb(kts[qd][pp][:, :CHUNK],
                  jnp.concatenate([hrows(v_sc[qd], 2 * pp), hrows(v_sc[qd], 2 * pp + 1)], axis=1))
            for qd in quads] for pp in range(2)]
    for qd in quads:
        for h in range(4):
            j = 4 * qd + h
            oh = hrows(o[qd], h)
            on = oh * lax.rsqrt(jnp.mean(oh * oh, axis=-1, keepdims=True) + EPS) * nw * _silu(head(z, j))
            o_ref[:, j * HEAD_DIM:(j + 1) * HEAD_DIM] = on.astype(o_ref.dtype)
    for qd in quads:
        for h in range(4):
            gl = glcol[qd][h * CHUNK:h * CHUNK + 1, :]
            s_ref[4 * qd + h] = s_old[4 * qd + h] * jnp.exp(gl) + head(upd[h // 2][qd], h % 2)


def _gdn_gate_rows(gb, gcum):
    nq = V_HEADS // 4
    to_quads = lambda x: x.reshape(BATCH, N_CHUNKS, CHUNK, nq, 4).transpose(0, 1, 3, 4, 2)
    g = to_quads(gcum[:N_PROMPT, V_HEADS:2 * V_HEADS])
    beta = to_quads(gb[:N_PROMPT, :V_HEADS])
    glast = jnp.broadcast_to(g[..., CHUNK - 1:], g.shape)
    rows = jnp.stack([g, beta, glast], axis=3).reshape(BATCH, N_CHUNKS, nq, 3, QUAD)
    return jnp.pad(rows, ((0, 0), (0, 0), (0, 0), (0, SUBLANES - 3), (0, 0)))


def _gdn_prompt(qkv, z, conv_w, gates, norm_w, *, layer):
    qw = GDN_PAIRS * HEAD_DIM
    vw = GDN_VH * HEAD_DIM
    kb0 = QK_W // qw
    vb0 = 2 * QK_W // vw
    rowblk = lambda b, h, c: b * N_CHUNKS + c
    return pl.pallas_call(
        _gdn_prompt_body,
        grid=(BATCH, GDN_GROUPS, N_CHUNKS),
        in_specs=[
            pl.BlockSpec((CHUNK, qw), lambda b, h, c: (rowblk(b, h, c), h)),
            pl.BlockSpec((CHUNK, qw), lambda b, h, c: (rowblk(b, h, c), kb0 + h)),
            pl.BlockSpec((CHUNK, vw), lambda b, h, c: (rowblk(b, h, c), vb0 + h)),
            pl.BlockSpec((None, CONV_W, qw), lambda b, h, c: (layer, 0, h)),
            pl.BlockSpec((None, CONV_W, qw), lambda b, h, c: (layer, 0, kb0 + h)),
            pl.BlockSpec((None, CONV_W, vw), lambda b, h, c: (layer, 0, vb0 + h)),
            pl.BlockSpec((CHUNK, vw), lambda b, h, c: (rowblk(b, h, c), h)),
            pl.BlockSpec((None, None, GDN_QUADS, SUBLANES, QUAD), lambda b, h, c: (b, c, h, 0, 0)),
            pl.BlockSpec((None, 1, HEAD_DIM), lambda b, h, c: (layer, 0, 0)),
            pl.BlockSpec((5, QUAD, QUAD), lambda b, h, c: (0, 0, 0)),
        ],
        out_specs=[
            pl.BlockSpec((CHUNK, vw), lambda b, h, c: (rowblk(b, h, c), h)),
            pl.BlockSpec((None, GDN_VH, HEAD_DIM, HEAD_DIM), lambda b, h, c: (b, h, 0, 0)),
        ],
        out_shape=[jax.ShapeDtypeStruct((N_PROMPT, V_W), BF16),
                   jax.ShapeDtypeStruct((BATCH, V_HEADS, HEAD_DIM, HEAD_DIM), F32)],
        scratch_shapes=[pltpu.VMEM((SUBLANES, qw), F32), pltpu.VMEM((SUBLANES, qw), F32),
                        pltpu.VMEM((SUBLANES, vw), F32)],
        compiler_params=_cparams(("arbitrary", "arbitrary", "arbitrary")),
        name="gdn_prompt",
    )(qkv, qkv, qkv, conv_w, conv_w, conv_w, z, gates, norm_w.reshape(DEPTH, 1, HEAD_DIM), _inv_masks())


def _gdn_sample_prep_body(x_ref, c_ref, w_ref, o_ref, *, normalize):
    w = w_ref[...]
    y = c_ref[0] * w[0:1]
    for j in range(1, CONV_W - 1):
        y = y + c_ref[j] * w[j:j + 1]
    y = y + x_ref[...] * w[CONV_W - 1:CONV_W]
    y = _silu(y)
    if normalize:
        is_q = pl.program_id(0) < QK_W // y.shape[1]
        scale = jnp.where(is_q, HEAD_DIM ** -0.5, 1.0)
        for h in range(y.shape[1] // HEAD_DIM):
            yh = y[:, h * HEAD_DIM:(h + 1) * HEAD_DIM]
            yh = yh * lax.rsqrt(jnp.sum(yh * yh, axis=-1, keepdims=True) + EPS) * scale
            o_ref[:, h * HEAD_DIM:(h + 1) * HEAD_DIM] = yh
    else:
        o_ref[...] = y


def _gdn_sample_prep(qkv, cache_t, conv_w, *, layer, col0, n, normalize):
    tn = 1024
    cb0 = col0 // tn
    rb = N_PROMPT // DEC_BATCH
    return pl.pallas_call(
        functools.partial(_gdn_sample_prep_body, normalize=normalize),
        grid=(n // tn,),
        in_specs=[
            pl.BlockSpec((DEC_BATCH, tn), lambda j: (rb, cb0 + j)),
            pl.BlockSpec((None, CONV_W - 1, DEC_BATCH, tn), lambda j: (layer, 0, 0, cb0 + j)),
            pl.BlockSpec((None, CONV_W, tn), lambda j: (layer, 0, cb0 + j)),
        ],
        out_specs=pl.BlockSpec((DEC_BATCH, tn), lambda j: (0, j)),
        out_shape=jax.ShapeDtypeStruct((DEC_BATCH, n), F32),
        compiler_params=_cparams(("arbitrary",)),
        name="gdn_sample_prep",
    )(qkv, cache_t, conv_w)


def _gdn_sample_body(s_ref, qt_ref, kt_ref, v_ref, z_ref, eg_ref, beta_ref, nw_ref, so_ref, o_ref):
    qt = qt_ref[...]
    kt = kt_ref[...]
    v = v_ref[...]
    z = z_ref[...]
    egs = eg_ref[...]
    betas = beta_ref[...]
    nw = nw_ref[...]
    for h in range(V_HEADS):
        hq = h // 2
        kc = kt[:, hq:hq + 1]
        qc = qt[:, hq:hq + 1]
        s = s_ref[h]
        eg = egs[:, h:h + 1]
        beta = betas[:, h:h + 1]
        ks = jnp.sum(s * kc, axis=0, keepdims=True)
        qs = jnp.sum(s * qc, axis=0, keepdims=True)
        vh = v[:, h * HEAD_DIM:(h + 1) * HEAD_DIM]
        v_new = beta * vh - (beta * eg) * ks
        qk = jnp.sum(qc * kc, axis=0, keepdims=True)
        o = eg * qs + qk * v_new
        so_ref[h] = s * eg + kc * v_new
        zz = z[:, h * HEAD_DIM:(h + 1) * HEAD_DIM]
        on = o * lax.rsqrt(jnp.mean(o * o, axis=-1, keepdims=True) + EPS) * nw * _silu(zz)
        o_ref[:, h * HEAD_DIM:(h + 1) * HEAD_DIM] = on.astype(o_ref.dtype)


def _gdn_sample(state, qt, kt, v, z, eg, beta, norm_w, *, layer):
    r3 = lambda w: pl.BlockSpec((None, 1, w), lambda b: (b, 0, 0))
    return pl.pallas_call(
        _gdn_sample_body,
        grid=(DEC_BATCH,),
        in_specs=[
            pl.BlockSpec((None, None, V_HEADS, HEAD_DIM, HEAD_DIM), lambda b: (layer, b, 0, 0, 0)),
            pl.BlockSpec((None, HEAD_DIM, QK_HEADS), lambda b: (b, 0, 0)),
            pl.BlockSpec((None, HEAD_DIM, QK_HEADS), lambda b: (b, 0, 0)),
            r3(V_W), r3(V_W), r3(V_HEADS), r3(V_HEADS),
            pl.BlockSpec((None, 1, HEAD_DIM), lambda b: (layer, 0, 0)),
        ],
        out_specs=[
            pl.BlockSpec((None, V_HEADS, HEAD_DIM, HEAD_DIM), lambda b: (b, 0, 0, 0)),
            r3(V_W),
        ],
        out_shape=[jax.ShapeDtypeStruct((DEC_BATCH, V_HEADS, HEAD_DIM, HEAD_DIM), F32),
                   jax.ShapeDtypeStruct((DEC_BATCH, 1, V_W), BF16)],
        compiler_params=_cparams(("arbitrary",)),
        name="gdn_sample",
    )(state, qt, kt, v, z, eg, beta, norm_w.reshape(DEPTH, 1, HEAD_DIM))


def _s5_in_proj(ub, wb_ref, re_ref, im_ref):
    for j in range(S5_NBLK):
        r = jnp.dot(ub[:, j * LANES:(j + 1) * LANES], wb_ref[j], preferred_element_type=F32)
        re_ref[:, j * S5_BS:(j + 1) * S5_BS] = r[:, :S5_BS]
        im_ref[:, j * S5_BS:(j + 1) * S5_BS] = r[:, S5_BS:]


def _s5_out_proj(u, re_ref, im_ref, wc_ref, d_ref, y_ref):
    for j in range(S5_NBLK):
        xr = re_ref[:, j * S5_BS:(j + 1) * S5_BS].astype(BF16)
        xi = im_ref[:, j * S5_BS:(j + 1) * S5_BS].astype(BF16)
        y = jnp.dot(xr, wc_ref[j, :S5_BS], preferred_element_type=F32)
        y = y + jnp.dot(xi, wc_ref[j, S5_BS:], preferred_element_type=F32)
        y = y + d_ref[:, j * LANES:(j + 1) * LANES] * u[:, j * LANES:(j + 1) * LANES]
        y_ref[:, j * LANES:(j + 1) * LANES] = _gelu_tanh(y)


def _s5_prompt_body(*refs):
    u_refs = refs[:BATCH]
    wb_ref, wc_ref, lr_ref, li_ref, d_ref, y_ref, sre_ref, sim_ref, re_ref, im_ref, utm_ref, ytm_ref = refs[BATCH:]

    @pl.when(pl.program_id(0) == 0)
    def _():
        sre_ref[...] = jnp.zeros_like(sre_ref)
        sim_ref[...] = jnp.zeros_like(sim_ref)
        utm_ref[...] = jnp.zeros_like(utm_ref)

    for b in range(BATCH):
        for j in range(S5_NBLK):
            utm_ref[j, pl.ds(b, S5_TT, stride=SUBLANES), :] = u_refs[b][:, j * LANES:(j + 1) * LANES]
    for j in range(S5_NBLK):
        r = jnp.dot(utm_ref[j].astype(BF16), wb_ref[j], preferred_element_type=F32)
        re_ref[:, j * S5_BS:(j + 1) * S5_BS] = r[:, :S5_BS]
        im_ref[:, j * S5_BS:(j + 1) * S5_BS] = r[:, S5_BS:]
    cw = 1024
    for cb in range(S5_STATES // cw):
        cols = slice(cb * cw, (cb + 1) * cw)
        lr = lr_ref[:, cols]
        li = li_ref[:, cols]

        def step(t, carry):
            xr, xi = carry
            rows = pl.ds(pl.multiple_of(t * SUBLANES, SUBLANES), SUBLANES)
            nr = lr * xr - li * xi + re_ref[rows, cols]
            ni = lr * xi + li * xr + im_ref[rows, cols]
            re_ref[rows, cols] = nr
            im_ref[rows, cols] = ni
            return nr, ni

        xr, xi = lax.fori_loop(0, S5_TT, step, (sre_ref[:, cols], sim_ref[:, cols]), unroll=2)
        sre_ref[:, cols] = xr
        sim_ref[:, cols] = xi
    for j in range(S5_NBLK):
        xr = re_ref[:, j * S5_BS:(j + 1) * S5_BS].astype(BF16)
        xi = im_ref[:, j * S5_BS:(j + 1) * S5_BS].astype(BF16)
        y = jnp.dot(xr, wc_ref[j, :S5_BS], preferred_element_type=F32)
        y = y + jnp.dot(xi, wc_ref[j, S5_BS:], preferred_element_type=F32)
        y = y + d_ref[:, j * LANES:(j + 1) * LANES] * utm_ref[j]
        ytm_ref[j] = _gelu_tanh(y)
    for b in range(BATCH):
        for j in range(S5_NBLK):
            y_ref[b, :, j * LANES:(j + 1) * LANES] = ytm_ref[j, pl.ds(b, S5_TT, stride=SUBLANES), :]


def _s5_prompt(tail, wb, wc, lam_re8, lam_im8, d_row, *, layer):
    rows = S5_TT * SUBLANES
    steps = SEQ // S5_TT
    full = lambda shape: pl.BlockSpec(shape, lambda t: (0,) * len(shape))
    lay = lambda shape: pl.BlockSpec((None,) + shape, lambda t: (layer,) + (0,) * len(shape))
    return pl.pallas_call(
        _s5_prompt_body,
        grid=(steps,),
        in_specs=[pl.BlockSpec((S5_TT, D_MODEL), functools.partial(lambda b, t: (b * steps + t, 0), b))
                  for b in range(BATCH)] + [
            lay((S5_NBLK, LANES, 2 * S5_BS)),
            lay((S5_NBLK, 2 * S5_BS, LANES)),
            lay((SUBLANES, S5_STATES)),
            lay((SUBLANES, S5_STATES)),
            lay((1, D_MODEL)),
        ],
        out_specs=[
            pl.BlockSpec((BATCH, S5_TT, D_MODEL), lambda t: (0, t, 0)),
            full((SUBLANES, S5_STATES)),
            full((SUBLANES, S5_STATES)),
        ],
        out_shape=[jax.ShapeDtypeStruct((BATCH, SEQ, D_MODEL), F32),
                   jax.ShapeDtypeStruct((SUBLANES, S5_STATES), F32),
                   jax.ShapeDtypeStruct((SUBLANES, S5_STATES), F32)],
        scratch_shapes=[pltpu.VMEM((rows, S5_STATES), F32), pltpu.VMEM((rows, S5_STATES), F32),
                        pltpu.VMEM((S5_NBLK, rows, LANES), F32), pltpu.VMEM((S5_NBLK, rows, LANES), F32)],
        compiler_params=_cparams(("arbitrary",)),
        name="s5_prompt",
    )(*([tail] * BATCH), wb, wc, lam_re8, lam_im8, d_row)


def _s5_sample_body(u_ref, x0r_ref, x0i_ref, wb_ref, wc_ref, lr_ref, li_ref, d_ref, y_ref, x1r_ref, x1i_ref):
    u = u_ref[...]
    _s5_in_proj(u.astype(BF16), wb_ref, x1r_ref, x1i_ref)
    lr = lr_ref[...]
    li = li_ref[...]
    xr = x0r_ref[...]
    xi = x0i_ref[...]
    x1r_ref[...] = x1r_ref[...] + (lr * xr - li * xi)
    x1i_ref[...] = x1i_ref[...] + (lr * xi + li * xr)
    _s5_out_proj(u, x1r_ref, x1i_ref, wc_ref, d_ref, y_ref)


def _s5_sample(u_all, x0r, x0i, wb, wc, lam_re, lam_im, d_row, *, layer):
    full = lambda shape: pl.BlockSpec(shape, lambda i: (0,) * len(shape))
    lay = lambda shape: pl.BlockSpec((None,) + shape, lambda i: (layer,) + (0,) * len(shape))
    st = lay((DEC_BATCH, S5_STATES))
    return pl.pallas_call(
        _s5_sample_body,
        grid=(1,),
        in_specs=[
            pl.BlockSpec((DEC_BATCH, D_MODEL), lambda i: (N_PROMPT // DEC_BATCH, 0)),
            st, st,
            lay((S5_NBLK, LANES, 2 * S5_BS)),
            lay((S5_NBLK, 2 * S5_BS, LANES)),
            lay((1, S5_STATES)), lay((1, S5_STATES)), lay((1, D_MODEL)),
        ],
        out_specs=[full((DEC_BATCH, D_MODEL)), full((DEC_BATCH, S5_STATES)), full((DEC_BATCH, S5_STATES))],
        out_shape=[jax.ShapeDtypeStruct((DEC_BATCH, D_MODEL), F32),
                   jax.ShapeDtypeStruct((DEC_BATCH, S5_STATES), F32),
                   jax.ShapeDtypeStruct((DEC_BATCH, S5_STATES), F32)],
        compiler_params=_cparams(("arbitrary",)),
        name="s5_sample",
    )(u_all, x0r, x0i, wb, wc, lam_re, lam_im, d_row)


def _s5_params(lam_re, lam_im, log_dt, b_re, b_im, c_re, c_im, d_skip):
    lam = lax.complex(lam_re, lam_im)
    dt = jnp.exp(log_dt)[:, None]
    lam_bar = jnp.exp(lam * dt)
    b_bar = ((lam_bar - 1.0) / lam)[..., None] * lax.complex(b_re, b_im)
    eye = jnp.eye(S5_BLK, dtype=F32)

    def in_blocks(b):
        b = b.reshape(S5_NBLK, S5_BLK, S5_N, S5_P)
        return jnp.einsum("jgnp,gh->jgphn", b, eye).reshape(S5_NBLK, S5_BLK * S5_P, S5_BS)

    def out_blocks(c):
        c = c.reshape(S5_NBLK, S5_BLK, S5_P, S5_N)
        return jnp.einsum("jgpn,gh->jgnhp", c, eye).reshape(S5_NBLK, S5_BS, S5_BLK * S5_P)

    wb = jnp.concatenate([in_blocks(b_bar.real), in_blocks(b_bar.imag)], axis=2).astype(BF16)
    wc = jnp.concatenate([out_blocks(c_re), -out_blocks(c_im)], axis=1).astype(BF16)
    return (wb, wc, lam_bar.real.reshape(1, S5_STATES), lam_bar.imag.reshape(1, S5_STATES),
            d_skip.reshape(1, D_MODEL))


def _router_body(h_ref, w_ref, b_ref, idx_ref, wt_ref):
    logits = jnp.dot(h_ref[...].astype(BF16), w_ref[...].astype(BF16), preferred_element_type=F32)
    scores = _sigmoid(logits)
    biased = scores + b_ref[...]
    lane = lax.broadcasted_iota(jnp.int32, biased.shape, 1)
    kcol = lax.broadcasted_iota(jnp.int32, (biased.shape[0], TOP_K), 1)
    idx = jnp.zeros((biased.shape[0], TOP_K), jnp.int32)
    sel = jnp.zeros((biased.shape[0], TOP_K), F32)
    for k in range(TOP_K):
        m = jnp.max(biased, axis=-1, keepdims=True)
        first = jnp.min(jnp.where(biased == m, lane, N_EXPERTS), axis=-1, keepdims=True)
        hit = lane == first
        s = jnp.sum(jnp.where(hit, scores, 0.0), axis=-1, keepdims=True)
        idx = jnp.where(kcol == k, first, idx)
        sel = jnp.where(kcol == k, s, sel)
        biased = jnp.where(hit, -jnp.inf, biased)
    idx_ref[...] = idx
    wt_ref[...] = sel / jnp.sum(sel, axis=-1, keepdims=True) * ROUTED_SCALE


def _router(h, w_router, router_bias, *, layer):
    tm = MM_TM
    return pl.pallas_call(
        _router_body,
        grid=(M_ROWS // tm,),
        in_specs=[
            pl.BlockSpec((tm, D_MODEL), lambda i: (i, 0)),
            pl.BlockSpec((None, D_MODEL, N_EXPERTS), lambda i: (layer, 0, 0)),
            pl.BlockSpec((None, 1, N_EXPERTS), lambda i: (layer, 0, 0)),
        ],
        out_specs=[pl.BlockSpec((tm, TOP_K), lambda i: (i, 0))] * 2,
        out_shape=[jax.ShapeDtypeStruct((M_ROWS, TOP_K), jnp.int32),
                   jax.ShapeDtypeStruct((M_ROWS, TOP_K), F32)],
        compiler_params=_cparams(("arbitrary",)),
        name="router",
    )(h, w_router, router_bias.reshape(DEPTH, 1, N_EXPERTS))


def _routing_tables(top_idx):
    flat_e = top_idx.reshape(-1)
    order = jnp.argsort(flat_e).astype(jnp.int32)
    experts = jnp.arange(N_EXPERTS, dtype=jnp.int32)
    sizes = jnp.sum((flat_e[:, None] == experts[None, :]).astype(jnp.int32), axis=0)
    starts = jnp.cumsum(sizes) - sizes
    padded = (sizes + MOE_BS - 1) // MOE_BS * MOE_BS
    pad_ends = jnp.cumsum(padded)
    pad_starts = pad_ends - padded
    i = jnp.arange(N_ASSIGN, dtype=jnp.int32)[:, None]
    mine = jnp.logical_and(i >= starts[None, :], i < (starts + sizes)[None, :])
    dest = i[:, 0] + jnp.sum(jnp.where(mine, (pad_starts - starts)[None, :], 0), axis=1)
    _, pos = lax.sort((order, dest.astype(jnp.int32)), num_keys=1)
    blk_start = jnp.arange(MOE_BLOCKS, dtype=jnp.int32) * MOE_BS
    block_e = jnp.minimum(jnp.sum(blk_start[:, None] >= pad_ends[None, :], axis=1), N_EXPERTS - 1).astype(jnp.int32)
    sorted_tok = jnp.concatenate([order // TOP_K, jnp.zeros((MOE_BS,), jnp.int32)])
    off = jnp.clip(starts[block_e] + blk_start - pad_starts[block_e], 0, N_ASSIGN)
    row_tok = jax.vmap(lambda o: lax.dynamic_slice(sorted_tok, (o,), (MOE_BS,)))(off)
    return row_tok, pos, block_e


def _to_slabs(x):
    return x.reshape(x.shape[0] * ROW_SLAB, LANES)


def _load_slab_rows(ref, n):
    return jnp.concatenate([ref[pl.ds(j, n, stride=SLAB_PITCH), :] for j in range(ROW_SLAB)], axis=1)


def _expert_gather(tok_ref, h_hbm, xbuf, sem, slot):
    for r in range(MOE_BS):
        src = h_hbm.at[pl.ds(pl.multiple_of(tok_ref[0, r] * ROW_SLAB, ROW_SLAB), ROW_SLAB), :]
        pltpu.make_async_copy(src, xbuf.at[slot, pl.ds(r * SLAB_PITCH, ROW_SLAB), :], sem.at[slot]).start()


def _expert_gather_wait(h_hbm, xbuf, sem, slot):
    pltpu.make_async_copy(h_hbm.at[pl.ds(0, MOE_BS * ROW_SLAB), :], xbuf.at[slot, pl.ds(0, MOE_BS * ROW_SLAB), :],
                          sem.at[slot]).wait()


def _expert_weight_copies(layer, e, wslot, w_hbm, wbuf, wsem):
    return [pltpu.make_async_copy(w.at[layer, e], buf.at[wslot], wsem.at[wslot]) for w, buf in zip(w_hbm, wbuf)]


def _experts_body(be_ref, first_ref, nxt_ref, wslot_ref, tok_ref, tokn_ref, h_hbm, wg_hbm, wu_hbm, wd_hbm, o_ref,
                  xbuf, sem, wgf, wuf, wdf, wsem, wgb, wub, wdb, *, layer):
    b = pl.program_id(0)
    nb = pl.num_programs(0)
    slot = lax.rem(b, 2)
    w_hbm = (wg_hbm, wu_hbm, wd_hbm)
    wbuf = (wgf, wuf, wdf)
    ws = wslot_ref[b]

    @pl.when(b == 0)
    def _():
        _expert_gather(tok_ref, h_hbm, xbuf, sem, 0)
        for c in _expert_weight_copies(layer, be_ref[0], 0, w_hbm, wbuf, wsem):
            c.start(priority=1)

    @pl.when(first_ref[b] == 1)
    def _():
        for c in _expert_weight_copies(layer, be_ref[b], ws, w_hbm, wbuf, wsem):
            c.wait()

        @pl.when(nxt_ref[b] >= 0)
        def _():
            for c in _expert_weight_copies(layer, nxt_ref[b], 1 - ws, w_hbm, wbuf, wsem):
                c.start(priority=1)

        wgb[...] = wgf[ws].astype(BF16)
        wub[...] = wuf[ws].astype(BF16)
        wdb[...] = wdf[ws].astype(BF16)

    _expert_gather_wait(h_hbm, xbuf, sem, slot)
    _expert_gather(tokn_ref, h_hbm, xbuf, sem, 1 - slot)
    x = _load_slab_rows(xbuf.at[slot], MOE_BS).astype(BF16)
    g = jnp.dot(x, wgb[...], preferred_element_type=F32)
    u = jnp.dot(x, wub[...], preferred_element_type=F32)
    a = (_silu(g) * u).astype(BF16)
    y = jnp.dot(a, wdb[...], preferred_element_type=F32)
    for j in range(ROW_SLAB):
        o_ref[pl.ds(j, MOE_BS, stride=ROW_SLAB), :] = y[:, j * LANES:(j + 1) * LANES]

    @pl.when(b == nb - 1)
    def _():
        _expert_gather_wait(h_hbm, xbuf, sem, 1 - slot)


def _experts(h_slabs, row_tok, block_e, w_gate, w_up, w_down, *, layer):
    tok3 = row_tok.reshape(MOE_BLOCKS, 1, MOE_BS)
    prev_e = jnp.concatenate([jnp.full((1,), -1, jnp.int32), block_e[:-1]])
    first = (block_e != prev_e).astype(jnp.int32)
    run = jnp.cumsum(first) - 1
    blk = jnp.arange(MOE_BLOCKS, dtype=jnp.int32)
    later_first = jnp.logical_and(first[None, :] == 1, blk[None, :] > blk[:, None])
    nxt_blk = jnp.min(jnp.where(later_first, blk[None, :], MOE_BLOCKS), axis=1)
    nxt = jnp.where(nxt_blk < MOE_BLOCKS, block_e[jnp.minimum(nxt_blk, MOE_BLOCKS - 1)], -1).astype(jnp.int32)
    wslot = (run % 2).astype(jnp.int32)

    smem_tok = lambda f: pl.BlockSpec((None, 1, MOE_BS), f, memory_space=pltpu.SMEM)
    any_spec = pl.BlockSpec(memory_space=pl.ANY)
    grid_spec = pltpu.PrefetchScalarGridSpec(
        num_scalar_prefetch=4,
        grid=(MOE_BLOCKS,),
        in_specs=[
            smem_tok(lambda b, *_: (b, 0, 0)),
            smem_tok(lambda b, *_: (jnp.minimum(b + 1, MOE_BLOCKS - 1), 0, 0)),
            any_spec, any_spec, any_spec, any_spec,
        ],
        out_specs=pl.BlockSpec((MOE_BS * ROW_SLAB, LANES), lambda b, *_: (b, 0)),
        scratch_shapes=[
            pltpu.VMEM((2, MOE_BS * SLAB_PITCH, LANES), F32),
            pltpu.SemaphoreType.DMA((2,)),
            pltpu.VMEM((2, D_MODEL, D_EXPERT), F32),
            pltpu.VMEM((2, D_MODEL, D_EXPERT), F32),
            pltpu.VMEM((2, D_EXPERT, D_MODEL), F32),
            pltpu.SemaphoreType.DMA((2,)),
            pltpu.VMEM((D_MODEL, D_EXPERT), BF16),
            pltpu.VMEM((D_MODEL, D_EXPERT), BF16),
            pltpu.VMEM((D_EXPERT, D_MODEL), BF16),
        ],
    )
    return pl.pallas_call(
        functools.partial(_experts_body, layer=layer),
        grid_spec=grid_spec,
        out_shape=jax.ShapeDtypeStruct((MOE_ROWS * ROW_SLAB, LANES), F32),
        compiler_params=_cparams(("arbitrary",)),
        name="experts",
    )(block_e, first, nxt, wslot, tok3, tok3, h_slabs, w_gate, w_up, w_down)


def _shared_body(h_ref, wg_ref, wu_ref, wd_ref, o_ref, wgb, wub, wdb):
    @pl.when(pl.program_id(0) == 0)
    def _():
        wgb[...] = wg_ref[...].astype(BF16)
        wub[...] = wu_ref[...].astype(BF16)
        wdb[...] = wd_ref[...].astype(BF16)

    x = h_ref[...].astype(BF16)
    g = jnp.dot(x, wgb[...], preferred_element_type=F32)
    u = jnp.dot(x, wub[...], preferred_element_type=F32)
    a = (_silu(g) * u).astype(BF16)
    o_ref[...] = jnp.dot(a, wdb[...], preferred_element_type=F32)


def _shared_expert(h, w_gate, w_up, w_down, *, layer):
    tm = MM_TM
    return pl.pallas_call(
        _shared_body,
        grid=(M_ROWS // tm,),
        in_specs=[
            pl.BlockSpec((tm, D_MODEL), lambda i: (i, 0)),
            pl.BlockSpec((None, D_MODEL, D_EXPERT), lambda i: (layer, 0, 0)),
            pl.BlockSpec((None, D_MODEL, D_EXPERT), lambda i: (layer, 0, 0)),
            pl.BlockSpec((None, D_EXPERT, D_MODEL), lambda i: (layer, 0, 0)),
        ],
        out_specs=pl.BlockSpec((tm, D_MODEL), lambda i: (i, 0)),
        out_shape=jax.ShapeDtypeStruct((M_ROWS, D_MODEL), F32),
        scratch_shapes=[pltpu.VMEM((D_MODEL, D_EXPERT), BF16), pltpu.VMEM((D_MODEL, D_EXPERT), BF16),
                        pltpu.VMEM((D_EXPERT, D_MODEL), BF16)],
        compiler_params=_cparams(("arbitrary",)),
        name="shared_expert",
    )(h, w_gate, w_up, w_down)


def _combine_gather(pos_ref, y_hbm, buf, sem, slot):
    def issue(t, carry):
        for k in range(TOP_K):
            p = pos_ref[0, t * TOP_K + k]
            src = y_hbm.at[pl.ds(pl.multiple_of(p * ROW_SLAB, ROW_SLAB), ROW_SLAB), :]
            dst = buf.at[slot, k, pl.ds(t * SLAB_PITCH, ROW_SLAB), :]
            pltpu.make_async_copy(src, dst, sem.at[slot]).start(priority=k % 2)
        return carry

    lax.fori_loop(0, COMB_T, issue, 0, unroll=2)


def _combine_gather_wait(y_hbm, buf, sem, slot):
    for k in range(TOP_K):
        pltpu.make_async_copy(y_hbm.at[pl.ds(0, COMB_T * ROW_SLAB), :],
                              buf.at[slot, k, pl.ds(0, COMB_T * ROW_SLAB), :], sem.at[slot]).wait()


def _combine_body(pos_ref, posn_ref, y_hbm, wt_ref, sh_ref, o_ref, buf, sem):
    i = pl.program_id(0)
    slot = lax.rem(i, 2)

    @pl.when(i == 0)
    def _():
        _combine_gather(pos_ref, y_hbm, buf, sem, 0)

    @pl.when(i + 1 < pl.num_programs(0))
    def _():
        _combine_gather(posn_ref, y_hbm, buf, sem, 1 - slot)

    _combine_gather_wait(y_hbm, buf, sem, slot)
    wt = wt_ref[...]
    wk = [jnp.broadcast_to(wt[:, k:k + 1], (COMB_T, LANES)) for k in range(TOP_K)]
    for j in range(ROW_SLAB):
        cols = slice(j * LANES, (j + 1) * LANES)
        acc = sh_ref[:, cols]
        for k in range(TOP_K):
            acc = acc + buf[slot, k, pl.ds(j, COMB_T, stride=SLAB_PITCH), :] * wk[k]
        o_ref[:, cols] = acc


def _combine(y_slabs, pos, top_w, shared):
    nt = M_ROWS // COMB_T
    pos3 = pos.reshape(nt, 1, COMB_T * TOP_K)
    smem_pos = lambda f: pl.BlockSpec((None, 1, COMB_T * TOP_K), f, memory_space=pltpu.SMEM)
    return pl.pallas_call(
        _combine_body,
        grid=(nt,),
        in_specs=[
            smem_pos(lambda i: (i, 0, 0)),
            smem_pos(lambda i: (jnp.minimum(i + 1, nt - 1), 0, 0)),
            pl.BlockSpec(memory_space=pl.ANY),
            pl.BlockSpec((COMB_T, TOP_K), lambda i: (i, 0)),
            pl.BlockSpec((COMB_T, D_MODEL), lambda i: (i, 0)),
        ],
        out_specs=pl.BlockSpec((COMB_T, D_MODEL), lambda i: (i, 0)),
        out_shape=jax.ShapeDtypeStruct((M_ROWS, D_MODEL), F32),
        scratch_shapes=[pltpu.VMEM((2, TOP_K, COMB_T * SLAB_PITCH, LANES), F32), pltpu.SemaphoreType.DMA((2,))],
        compiler_params=_cparams(("arbitrary",)),
        name="combine",
    )(pos3, pos3, y_slabs, top_w, shared)


def _mixer(l, h1, p, cache_t, state_gdn, s5_x0r, s5_x0i):
    w_in = p["w_in"]
    qkv = _mm(h1, w_in, layer=l, col0=0, n=QKV_W, tn=1024, name="in_qkv")
    z = _mm(h1, w_in, layer=l, col0=COL_Z, n=V_W, tn=1024, name="in_z")
    ba = _mm(h1, p["w_ba"], layer=l, col0=0, n=LANES, tn=LANES, name="in_ba")
    tail = _mm(h1, p["w_tail"], layer=l, col0=0, n=3 * D_MODEL, tn=1024, name="in_tail")

    gb, gcum = _gates(ba, p["alog_row"][l], p["dtb_row"][l])
    o_p, gdn_p = _gdn_prompt(qkv, z, p["conv_w"], _gdn_gate_rows(gb, gcum), p["gdn_norm_w"], layer=l)

    qk_s = _gdn_sample_prep(qkv, cache_t, p["conv_w"], layer=l, col0=0, n=2 * QK_W, normalize=True)
    v_s = _gdn_sample_prep(qkv, cache_t, p["conv_w"], layer=l, col0=2 * QK_W, n=V_W, normalize=False)
    col_form = lambda x: x.reshape(DEC_BATCH, QK_HEADS, HEAD_DIM).transpose(0, 2, 1)
    qt = col_form(qk_s[:, :QK_W])
    kt = col_form(qk_s[:, QK_W:])
    beta_s = gb[N_PROMPT:, :V_HEADS].reshape(DEC_BATCH, 1, V_HEADS)
    eg_s = jnp.exp(gb[N_PROMPT:, V_HEADS:2 * V_HEADS]).reshape(DEC_BATCH, 1, V_HEADS)
    z_s = z[N_PROMPT:].reshape(DEC_BATCH, 1, V_W)
    gdn_s, o_s = _gdn_sample(state_gdn, qt, kt, v_s.reshape(DEC_BATCH, 1, V_W), z_s, eg_s, beta_s,
                             p["gdn_norm_w"], layer=l)
    o_all = jnp.concatenate([o_p, o_s.reshape(DEC_BATCH, V_W)], axis=0)
    y_a = _mm(o_all, p["w_br_a"], layer=l, col0=0, n=D_MODEL, tn=512, name="br_a")

    wb, wc, lam_re, lam_im, d_row = p["s5"]
    lam_re8 = jnp.broadcast_to(lam_re, (DEPTH, SUBLANES, S5_STATES))
    lam_im8 = jnp.broadcast_to(lam_im, (DEPTH, SUBLANES, S5_STATES))
    y_p, re_p, im_p = _s5_prompt(tail, wb, wc, lam_re8, lam_im8, d_row, layer=l)
    y_s, re_s, im_s = _s5_sample(tail, s5_x0r, s5_x0i, wb, wc, lam_re, lam_im, d_row, layer=l)
    y_all = jnp.concatenate([y_p.reshape(N_PROMPT, D_MODEL), y_s], axis=0)
    ys_spec = pl.BlockSpec((MM_TM, 1024), lambda j, i: (i, j))
    y_glu = _mm(y_all, p["w_glu"], layer=l, col0=0, n=D_MODEL, tn=1024, out_dtype=BF16, epilogue="glu",
                extra=(y_all,), extra_specs=(ys_spec,), name="glu")

    tn = 1024
    nb = D_MODEL // tn
    merged = _mm(y_glu, p["w_br_b"], layer=l, col0=0, n=D_MODEL, tn=tn, out_dtype=BF16, epilogue="merge",
                 extra=(y_a, tail, tail),
                 extra_specs=(pl.BlockSpec((MM_TM, tn), lambda j, i: (i, j)),
                              pl.BlockSpec((MM_TM, tn), lambda j, i: (i, nb + j)),
                              pl.BlockSpec((MM_TM, tn), lambda j, i: (i, 2 * nb + j))),
                 name="br_b_merge")
    mix = _mm(merged, p["w_out"], layer=l, col0=0, n=D_MODEL, tn=1024, name="out_proj")

    conv_p = jnp.stack([qkv[(b + 1) * SEQ - (CONV_W - 1):(b + 1) * SEQ] for b in range(BATCH)])
    conv_s = jnp.concatenate([cache_t[l, 1:].transpose(1, 0, 2), qkv[N_PROMPT:, None, :]], axis=1)
    states = (conv_p, gdn_p,
              re_p[:BATCH].reshape(BATCH, S5_G, S5_N), im_p[:BATCH].reshape(BATCH, S5_G, S5_N),
              conv_s, gdn_s,
              re_s.reshape(DEC_BATCH, S5_G, S5_N), im_s.reshape(DEC_BATCH, S5_G, S5_N))
    return mix, states


def _moe(l, h2, p):
    top_idx, top_w = _router(h2, p["w_router"], p["router_bias"], layer=l)
    row_tok, pos, block_e = _routing_tables(top_idx)
    y_slabs = _experts(_to_slabs(h2), row_tok, block_e, p["w_exp_gate"], p["w_exp_up"], p["w_exp_down"], layer=l)
    shared = _shared_expert(h2, p["w_sh_gate"], p["w_sh_up"], p["w_sh_down"], layer=l)
    return _combine(y_slabs, pos, top_w, shared)


def kernel(x_prompt, x_sample, cache_conv, state_gdn, state_s5_re, state_s5_im, c_prompt, c_sample, ln_in_g, ln_in_b, w_ada, b_ada, w_in, conv_w, gdn_a_log, gdn_dt_bias, gdn_norm_w, s5_lam_re, s5_lam_im, s5_log_dt, s5_b_re, s5_b_im, s5_c_re, s5_c_im, s5_d, w_glu, w_br_a, w_br_b, w_out, ln1_g, ln1_b, w_router, router_bias, w_exp_gate, w_exp_up, w_exp_down, w_sh_gate, w_sh_up, w_sh_down, ln2_g, ln2_b):
    c_all = jnp.concatenate([c_prompt, jnp.zeros((SUBLANES - BATCH, D_MODEL), F32), c_sample], axis=0)
    mod = _ada_all(c_all, w_ada, b_ada)
    mod_p = mod[:, :SUBLANES]
    mod_s = mod[:, SUBLANES:]

    lane_pad = lambda a: jnp.pad(a, ((0, 0), (V_HEADS, LANES - 2 * V_HEADS)))[:, None, :]
    p = {
        "w_in": w_in,
        "w_ba": jnp.pad(w_in[:, :, COL_BA:COL_TAIL], ((0, 0), (0, 0), (0, LANES - 2 * V_HEADS))),
        "w_tail": w_in[:, :, COL_TAIL:],
        "alog_row": lane_pad(gdn_a_log), "dtb_row": lane_pad(gdn_dt_bias),
        "conv_w": conv_w, "gdn_norm_w": gdn_norm_w,
        "s5": jax.vmap(_s5_params)(s5_lam_re, s5_lam_im, s5_log_dt, s5_b_re, s5_b_im, s5_c_re, s5_c_im, s5_d),
        "w_glu": w_glu, "w_br_a": w_br_a, "w_br_b": w_br_b, "w_out": w_out,
        "w_router": w_router, "router_bias": router_bias,
        "w_exp_gate": w_exp_gate, "w_exp_up": w_exp_up, "w_exp_down": w_exp_down,
        "w_sh_gate": w_sh_gate, "w_sh_up": w_sh_up, "w_sh_down": w_sh_down,
    }
    cache_t = cache_conv.transpose(0, 2, 1, 3)
    s5_x0r = state_s5_re.reshape(DEPTH, DEC_BATCH, S5_STATES)
    s5_x0i = state_s5_im.reshape(DEPTH, DEC_BATCH, S5_STATES)

    x_all = jnp.concatenate([x_prompt.reshape(N_PROMPT, D_MODEL), x_sample.reshape(DEC_BATCH, D_MODEL)], axis=0)
    x, h1 = _ln_in(x_all, ln_in_g, ln_in_b, mod_p, mod_s)

    per_layer = []
    for l in range(DEPTH):
        mix, states = _mixer(l, h1, p, cache_t, state_gdn, s5_x0r, s5_x0i)
        per_layer.append(states)
        x, h2 = _deepnorm(x, mix, ln1_g, ln1_b, mod_p, mod_s, layer=l, gate_comp=2, mod_layer=l,
                          sc_comp=4, sh_comp=3, h_dtype=F32)
        ffn = _moe(l, h2, p)
        nxt = min(l + 1, DEPTH - 1)
        x, h1 = _deepnorm(x, ffn, ln2_g, ln2_b, mod_p, mod_s, layer=l, gate_comp=5, mod_layer=nxt,
                          sc_comp=1, sh_comp=0, h_dtype=BF16)

    stacked = [jnp.stack([s[i] for s in per_layer]) for i in range(8)]
    return (x[:N_PROMPT].reshape(BATCH, SEQ, D_MODEL), x[N_PROMPT:].reshape(DEC_BATCH, 1, D_MODEL), *stacked)
```

```python
import functools
import math

import jax
import jax.numpy as jnp
from jax import lax
from jax.experimental import pallas as pl
from jax.experimental.pallas import tpu as pltpu

F32 = jnp.float32
BF16 = jnp.bfloat16

D_MODEL = 2048
BATCH = 4
SEQ = 2048
DEPTH = 4
DEC_BATCH = 128
N_PROMPT = BATCH * SEQ
M_ROWS = N_PROMPT + DEC_BATCH

QK_HEADS = 16
V_HEADS = 32
HEAD_DIM = 128
QK_W = QK_HEADS * HEAD_DIM
V_W = V_HEADS * HEAD_DIM
QKV_W = 2 * QK_W + V_W
CONV_W = 4
CHUNK = 64
N_CHUNKS = SEQ // CHUNK

S5_P = 16
S5_N = 64
S5_G = D_MODEL // S5_P
S5_STATES = S5_G * S5_N

N_EXPERTS = 64
TOP_K = 8
D_EXPERT = D_MODEL // 4
ROUTED_SCALE = 2.5
N_ASSIGN = M_ROWS * TOP_K

ALPHA = (2 * DEPTH) ** 0.25
EPS = 1e-6

COL_Z = QKV_W
COL_BA = QKV_W + V_W
COL_TAIL = COL_BA + 2 * V_HEADS

LANES = 128
SUBLANES = 8
VMEM_LIMIT = 56 * 1024 * 1024

ROW_TILE = 128
N_ROW_TILES = M_ROWS // ROW_TILE
PROMPT_ROW_TILES = N_PROMPT // ROW_TILE
TILES_PER_SEQ = SEQ // ROW_TILE
MM_TM = 640

QUAD = 4 * CHUNK
GDN_QUADS = 4
GDN_PAIRS = 2 * GDN_QUADS
GDN_GROUPS = QK_HEADS // GDN_PAIRS
GDN_VH = 2 * GDN_PAIRS

S5_TT = 32
S5_BLK = 8
S5_NBLK = S5_G // S5_BLK
S5_BS = S5_BLK * S5_N

ROW_SLAB = D_MODEL // LANES
SLAB_PITCH = ROW_SLAB + 4
MOE_BS = 256
MOE_BLOCKS = -(-N_ASSIGN // MOE_BS) + N_EXPERTS
MOE_ROWS = MOE_BLOCKS * MOE_BS
COMB_T = 128


def _cparams(sem, vmem=VMEM_LIMIT):
    return pltpu.CompilerParams(dimension_semantics=sem, vmem_limit_bytes=vmem)


def _sigmoid(x):
    return jax.nn.sigmoid(x)


def _silu(x):
    return x * jax.nn.sigmoid(x)


def _gelu_tanh(x):
    c = math.sqrt(2.0 / math.pi)
    return 0.5 * x * (1.0 + jnp.tanh(c * (x + 0.044715 * (x * x * x))))


def _layer_norm(x, g, b):
    mu = jnp.mean(x, axis=-1, keepdims=True)
    xc = x - mu
    var = jnp.mean(xc * xc, axis=-1, keepdims=True)
    return xc * lax.rsqrt(var + EPS) * g + b


def _mm_body(*refs, prologue, epilogue, n_extra):
    a_ref, w_ref = refs[0], refs[1]
    extra = refs[2:2 + n_extra]
    o_ref = refs[2 + n_extra]
    wbf_ref = refs[3 + n_extra]

    @pl.when(pl.program_id(1) == 0)
    def _():
        wbf_ref[...] = w_ref[...].astype(BF16)

    a = a_ref[...]
    if prologue == "silu":
        a = _silu(a.astype(F32))
    acc = jnp.dot(a.astype(BF16), wbf_ref[...], preferred_element_type=F32)
    if epilogue == "bias":
        acc = acc + extra[0][...]
    elif epilogue == "glu":
        acc = extra[0][...].astype(F32) * _sigmoid(acc)
    elif epilogue == "merge":
        ya, ga, gb = extra[0][...], extra[1][...], extra[2][...]
        acc = _sigmoid(ga) * ya + _sigmoid(gb) * acc
    o_ref[...] = acc.astype(o_ref.dtype)


def _mm(a, w, *, layer, col0, n, tn, tm=MM_TM, out_dtype=F32, prologue=None, epilogue=None,
        extra=(), extra_specs=(), rows=None, name="mm"):
    rows = a.shape[0] if rows is None else rows
    k = a.shape[1]
    assert rows % tm == 0 and n % tn == 0 and col0 % tn == 0
    cb0 = col0 // tn
    grid = (n // tn, rows // tm)
    in_specs = [
        pl.BlockSpec((tm, k), lambda j, i: (i, 0)),
        pl.BlockSpec((None, k, tn), lambda j, i: (layer, 0, cb0 + j)),
    ] + list(extra_specs)
    return pl.pallas_call(
        functools.partial(_mm_body, prologue=prologue, epilogue=epilogue, n_extra=len(extra)),
        grid=grid,
        in_specs=in_specs,
        out_specs=pl.BlockSpec((tm, tn), lambda j, i: (i, j)),
        out_shape=jax.ShapeDtypeStruct((rows, n), out_dtype),
        scratch_shapes=[pltpu.VMEM((k, tn), BF16)],
        compiler_params=_cparams(("arbitrary", "arbitrary")),
        name=name,
    )(a, w, *extra)


def _ada_body(c_ref, w_ref, b_ref, o_ref):
    a = _silu(c_ref[...]).astype(BF16)
    o_ref[...] = jnp.dot(a, w_ref[...].astype(BF16), preferred_element_type=F32) + b_ref[...]


def _ada_all(c_all, w_ada, b_ada):
    rows = c_all.shape[0]
    tn = 1024
    nj = (6 * D_MODEL) // tn
    return pl.pallas_call(
        _ada_body,
        grid=(DEPTH * nj,),
        in_specs=[
            pl.BlockSpec((rows, D_MODEL), lambda g: (0, 0)),
            pl.BlockSpec((None, D_MODEL, tn), lambda g: (g // nj, 0, g % nj)),
            pl.BlockSpec((None, 1, tn), lambda g: (g // nj, 0, g % nj)),
        ],
        out_specs=pl.BlockSpec((None, rows, tn), lambda g: (g // nj, 0, g % nj)),
        out_shape=jax.ShapeDtypeStruct((DEPTH, rows, 6 * D_MODEL), F32),
        compiler_params=_cparams(("arbitrary",)),
        name="ada",
    )(c_all, w_ada, b_ada.reshape(DEPTH, 1, 6 * D_MODEL))


def _mod_pick(i, mp_ref, ms_ref):
    b = jnp.minimum(i // TILES_PER_SEQ, BATCH - 1)
    row = mp_ref[pl.ds(b, 1), :]
    return jnp.where(i < PROMPT_ROW_TILES, row, ms_ref[...])


def _mod_specs(layer, comp):
    return [
        pl.BlockSpec((None, SUBLANES, D_MODEL), lambda i: (layer, 0, comp)),
        pl.BlockSpec((None, DEC_BATCH, D_MODEL), lambda i: (layer, 0, comp)),
    ]


def _row_spec():
    return pl.BlockSpec((ROW_TILE, D_MODEL), lambda i: (i, 0))


def _vec_spec():
    return pl.BlockSpec((1, D_MODEL), lambda i: (0, 0))


def _ln_in_body(x_ref, g_ref, b_ref, scp, scs, shp, shs, x_out, h_out):
    i = pl.program_id(0)
    x = _layer_norm(x_ref[...], g_ref[...], b_ref[...])
    x_out[...] = x
    h = x * (1.0 + _mod_pick(i, scp, scs)) + _mod_pick(i, shp, shs)
    h_out[...] = h.astype(h_out.dtype)


def _ln_in(x_all, g, b, mod_p, mod_s):
    return pl.pallas_call(
        _ln_in_body,
        grid=(N_ROW_TILES,),
        in_specs=[_row_spec(), _vec_spec(), _vec_spec()] + _mod_specs(0, 1) + _mod_specs(0, 0),
        out_specs=[_row_spec(), _row_spec()],
        out_shape=[jax.ShapeDtypeStruct((M_ROWS, D_MODEL), F32),
                   jax.ShapeDtypeStruct((M_ROWS, D_MODEL), BF16)],
        compiler_params=_cparams(("arbitrary",)),
        name="ln_in",
    )(x_all, g.reshape(1, -1), b.reshape(1, -1), mod_p, mod_s, mod_p, mod_s)


def _deepnorm_body(x_ref, y_ref, g_ref, b_ref, gtp, gts, scp, scs, shp, shs, x_out, h_out):
    i = pl.program_id(0)
    r = ALPHA * x_ref[...] + _mod_pick(i, gtp, gts) * y_ref[...]
    x = _layer_norm(r, g_ref[...], b_ref[...])
    x_out[...] = x
    h = x * (1.0 + _mod_pick(i, scp, scs)) + _mod_pick(i, shp, shs)
    h_out[...] = h.astype(h_out.dtype)


def _deepnorm(x, y, g, b, mod_p, mod_s, *, layer, gate_comp, mod_layer, sc_comp, sh_comp, h_dtype):
    return pl.pallas_call(
        _deepnorm_body,
        grid=(N_ROW_TILES,),
        in_specs=[_row_spec(), _row_spec(),
                  pl.BlockSpec((None, 1, D_MODEL), lambda i: (layer, 0, 0)),
                  pl.BlockSpec((None, 1, D_MODEL), lambda i: (layer, 0, 0))]
        + _mod_specs(layer, gate_comp) + _mod_specs(mod_layer, sc_comp) + _mod_specs(mod_layer, sh_comp),
        out_specs=[_row_spec(), _row_spec()],
        out_shape=[jax.ShapeDtypeStruct((M_ROWS, D_MODEL), F32),
                   jax.ShapeDtypeStruct((M_ROWS, D_MODEL), h_dtype)],
        compiler_params=_cparams(("arbitrary",)),
        name="deepnorm",
    )(x, y, g.reshape(DEPTH, 1, D_MODEL), b.reshape(DEPTH, 1, D_MODEL),
      mod_p, mod_s, mod_p, mod_s, mod_p, mod_s)


def _gates_body(ba_ref, alog_ref, dtb_ref, gb_ref, gc_ref):
    x = ba_ref[...]
    lane = lax.broadcasted_iota(jnp.int32, x.shape, 1)
    beta = _sigmoid(x)
    z = x + dtb_ref[...]
    softplus = jnp.maximum(z, 0.0) + jnp.log1p(jnp.exp(-jnp.abs(z)))
    g = -jnp.exp(alog_ref[...]) * softplus
    is_beta = lane < V_HEADS
    is_g = jnp.logical_and(lane >= V_HEADS, lane < 2 * V_HEADS)
    g = jnp.where(is_g, g, 0.0)
    gb_ref[...] = jnp.where(is_beta, beta, g)
    r = lax.broadcasted_iota(jnp.int32, (ROW_TILE, ROW_TILE), 0)
    c = lax.broadcasted_iota(jnp.int32, (ROW_TILE, ROW_TILE), 1)
    tri = jnp.logical_and(c <= r, (r // CHUNK) == (c // CHUNK)).astype(F32)
    gc_ref[...] = jnp.dot(tri, g, preferred_element_type=F32, precision=lax.Precision.HIGHEST)


def _gates(ba, alog_row, dtb_row):
    spec = pl.BlockSpec((ROW_TILE, LANES), lambda i: (i, 0))
    vec = pl.BlockSpec((1, LANES), lambda i: (0, 0))
    return pl.pallas_call(
        _gates_body,
        grid=(N_ROW_TILES,),
        in_specs=[spec, vec, vec],
        out_specs=[spec, spec],
        out_shape=[jax.ShapeDtypeStruct((M_ROWS, LANES), F32)] * 2,
        compiler_params=_cparams(("arbitrary",)),
        name="gates",
    )(ba, alog_row, dtb_row)


def _dotb(x, y):
    return jnp.dot(x.astype(BF16), y.astype(BF16), preferred_element_type=F32)


def _inv_masks():
    r = jnp.arange(QUAD)[:, None]
    c = jnp.arange(QUAD)[None, :]
    same = lambda sh: (r >> sh) == (c >> sh)
    masks = [same(3)] + [jnp.logical_and(same(sh), jnp.logical_not(same(sh - 1))) for sh in (4, 5, 6)] + [r == c]
    return jnp.stack(masks).astype(BF16)


def _inv_unit_lower(mats, m_ref):
    ds = [a * m_ref[0] for a in mats]
    d2s = [_dotb(d, d).astype(BF16) for d in ds]
    d4s = [_dotb(d2, d2).astype(BF16) for d2 in d2s]
    ts = [m_ref[4].astype(F32) - d.astype(F32) for d in ds]
    ts = [t + _dotb(t, d2) for t, d2 in zip(ts, d2s)]
    ts = [t + _dotb(t, d4) for t, d4 in zip(ts, d4s)]
    for lvl in (1, 2, 3):
        tbs = [t.astype(BF16) for t in ts]
        ets = [_dotb(a * m_ref[lvl], tb).astype(BF16) for a, tb in zip(mats, tbs)]
        ts = [t - _dotb(tb, et) for t, tb, et in zip(ts, tbs, ets)]
    return ts


def _conv_silu(x_ref, hist_ref, w_ref):
    x = x_ref[...]
    w = w_ref[...]
    xc = jnp.concatenate([hist_ref[...], x], axis=0)
    base = SUBLANES - (CONV_W - 1)
    y = xc[base:base + CHUNK] * w[0:1]
    for j in range(1, CONV_W - 1):
        y = y + xc[base + j:base + j + CHUNK] * w[j:j + 1]
    y = y + x * w[CONV_W - 1:CONV_W]
    hist_ref[...] = x[CHUNK - SUBLANES:CHUNK]
    return _silu(y)


def _gdn_prompt_body(q_ref, k_ref, v_ref, wq_ref, wk_ref, wv_ref, z_ref, gates_ref, nw_ref, m_ref, o_ref, s_ref,
                     hq_ref, hk_ref, hv_ref):
    c = pl.program_id(2)

    @pl.when(c == 0)
    def _():
        s_ref[...] = jnp.zeros_like(s_ref)
        hq_ref[...] = jnp.zeros_like(hq_ref)
        hk_ref[...] = jnp.zeros_like(hk_ref)
        hv_ref[...] = jnp.zeros_like(hv_ref)

    s_old = [s_ref[j] for j in range(GDN_VH)]
    qa = _conv_silu(q_ref, hq_ref, wq_ref)
    ka = _conv_silu(k_ref, hk_ref, wk_ref)
    va = _conv_silu(v_ref, hv_ref, wv_ref)
    z = z_ref[...]
    nw = nw_ref[...]
    head = lambda x, j: x[:, j * HEAD_DIM:(j + 1) * HEAD_DIM]

    pr = 2 * CHUNK
    row = lax.broadcasted_iota(jnp.int32, (pr, pr), 0)
    col = lax.broadcasted_iota(jnp.int32, (pr, pr), 1)
    same_head = (row >> 6) == (col >> 6)
    causal = jnp.logical_and(same_head, row >= col)
    strict = jnp.logical_and(same_head, row > col)
    zero_b = jnp.zeros((pr, pr), BF16)

    blockdiag = lambda x: jnp.concatenate([jnp.concatenate([x[0], zero_b], axis=1),
                                           jnp.concatenate([zero_b, x[1]], axis=1)], axis=0)
    quads = range(GDN_QUADS)
    hrows = lambda x, h: x[h * CHUNK:(h + 1) * CHUNK]

    gcol, bcol, glcol, eg, kst, qst, vst, kts, a_mat, qkd = [], [], [], [], [], [], [], [], [], []
    for qd in quads:
        gates = gates_ref[qd]
        gcols = gates.T
        gcol.append(gcols[:, 0:1])
        bcol.append(gcols[:, 1:2])
        glcol.append(gcols[:, 2:3])
        grow = gates[0:1, :]
        eg.append(jnp.exp(gcol[qd]))
        kps, qps, kpts, a_blk, qkd_blk = [], [], [], [], []
        for pp in range(2):
            q = head(qa, 2 * qd + pp)
            k = head(ka, 2 * qd + pp)
            qn = q * lax.rsqrt(jnp.sum(q * q, axis=-1, keepdims=True) + EPS) * (HEAD_DIM ** -0.5)
            kn = k * lax.rsqrt(jnp.sum(k * k, axis=-1, keepdims=True) + EPS)
            kp = jnp.concatenate([kn, kn], axis=0)
            qp = jnp.concatenate([qn, qn], axis=0)
            kp_t = kp.T.astype(BF16)
            rs = slice(pp * pr, (pp + 1) * pr)
            dec = jnp.exp(jnp.where(causal, gcol[qd][rs] - grow[:, rs], -jnp.inf))
            a_blk.append(jnp.where(strict, bcol[qd][rs] * _dotb(kp, kp_t) * dec, 0.0).astype(BF16))
            qkd_blk.append((_dotb(qp, kp_t) * dec).astype(BF16))
            kps.append(kp)
            qps.append(qp)
            kpts.append(kp_t)
        kst.append(jnp.concatenate(kps, axis=0))
        qst.append(jnp.concatenate(qps, axis=0))
        vst.append(jnp.concatenate([head(va, 4 * qd + h) for h in range(4)], axis=0))
        kts.append(kpts)
        a_mat.append(blockdiag(a_blk))
        qkd.append(blockdiag(qkd_blk))

    t = _inv_unit_lower(a_mat, m_ref)
    sol = [_dotb(t[qd], jnp.concatenate([vst[qd] * bcol[qd], kst[qd] * (bcol[qd] * eg[qd])], axis=1)) for qd in quads]
    qdec = [qst[qd] * eg[qd] for qd in quads]
    res = [[_dotb(jnp.concatenate([hrows(sol[qd][:, HEAD_DIM:], h), hrows(qdec[qd], h)], axis=0), s_old[4 * qd + h])
            for qd in quads] for h in range(4)]
    v_new = [jnp.concatenate([hrows(sol[qd][:, :HEAD_DIM], h) - res[h][qd][:CHUNK] for h in range(4)], axis=0)
             for qd in quads]
    o = [jnp.concatenate([res[h][qd][CHUNK:] for h in range(4)], axis=0) + _dotb(qkd[qd], v_new[qd]) for qd in quads]
    v_sc = [v_new[qd] * jnp.exp(glcol[qd] - gcol[qd]) for qd in quads]
    upd = [[_dotb(kts[qd][pp][:, :CHUNK],
                  jnp.concatenate([hrows(v_sc[qd], 2 * pp), hrows(v_sc[qd], 2 * pp + 1)], axis=1))
            for qd in quads] for pp in range(2)]
    for qd in quads:
        for h in range(4):
            j = 4 * qd + h
            oh = hrows(o[qd], h)
            on = oh * lax.rsqrt(jnp.mean(oh * oh, axis=-1, keepdims=True) + EPS) * nw * _silu(head(z, j))
            o_ref[:, j * HEAD_DIM:(j + 1) * HEAD_DIM] = on.astype(o_ref.dtype)
    for qd in quads:
        for h in range(4):
            gl = glcol[qd][h * CHUNK:h * CHUNK + 1, :]
            s_ref[4 * qd + h] = s_old[4 * qd + h] * jnp.exp(gl) + head(upd[h // 2][qd], h % 2)


def _gdn_gate_rows(gb, gcum):
    nq = V_HEADS // 4
    to_quads = lambda x: x.reshape(BATCH, N_CHUNKS, CHUNK, nq, 4).transpose(0, 1, 3, 4, 2)
    g = to_quads(gcum[:N_PROMPT, V_HEADS:2 * V_HEADS])
    beta = to_quads(gb[:N_PROMPT, :V_HEADS])
    glast = jnp.broadcast_to(g[..., CHUNK - 1:], g.shape)
    rows = jnp.stack([g, beta, glast], axis=3).reshape(BATCH, N_CHUNKS, nq, 3, QUAD)
    return jnp.pad(rows, ((0, 0), (0, 0), (0, 0), (0, SUBLANES - 3), (0, 0)))


def _gdn_prompt(qkv, z, conv_w, gates, norm_w, *, layer):
    qw = GDN_PAIRS * HEAD_DIM
    vw = GDN_VH * HEAD_DIM
    kb0 = QK_W // qw
    vb0 = 2 * QK_W // vw
    rowblk = lambda b, h, c: b * N_CHUNKS + c
    return pl.pallas_call(
        _gdn_prompt_body,
        grid=(BATCH, GDN_GROUPS, N_CHUNKS),
        in_specs=[
            pl.BlockSpec((CHUNK, qw), lambda b, h, c: (rowblk(b, h, c), h)),
            pl.BlockSpec((CHUNK, qw), lambda b, h, c: (rowblk(b, h, c), kb0 + h)),
            pl.BlockSpec((CHUNK, vw), lambda b, h, c: (rowblk(b, h, c), vb0 + h)),
            pl.BlockSpec((None, CONV_W, qw), lambda b, h, c: (layer, 0, h)),
            pl.BlockSpec((None, CONV_W, qw), lambda b, h, c: (layer, 0, kb0 + h)),
            pl.BlockSpec((None, CONV_W, vw), lambda b, h, c: (layer, 0, vb0 + h)),
            pl.BlockSpec((CHUNK, vw), lambda b, h, c: (rowblk(b, h, c), h)),
            pl.BlockSpec((None, None, GDN_QUADS, SUBLANES, QUAD), lambda b, h, c: (b, c, h, 0, 0)),
            pl.BlockSpec((None, 1, HEAD_DIM), lambda b, h, c: (layer, 0, 0)),
            pl.BlockSpec((5, QUAD, QUAD), lambda b, h, c: (0, 0, 0)),
        ],
        out_specs=[
            pl.BlockSpec((CHUNK, vw), lambda b, h, c: (rowblk(b, h, c), h)),
            pl.BlockSpec((None, GDN_VH, HEAD_DIM, HEAD_DIM), lambda b, h, c: (b, h, 0, 0)),
        ],
        out_shape=[jax.ShapeDtypeStruct((N_PROMPT, V_W), BF16),
                   jax.ShapeDtypeStruct((BATCH, V_HEADS, HEAD_DIM, HEAD_DIM), F32)],
        scratch_shapes=[pltpu.VMEM((SUBLANES, qw), F32), pltpu.VMEM((SUBLANES, qw), F32),
                        pltpu.VMEM((SUBLANES, vw), F32)],
        compiler_params=_cparams(("arbitrary", "arbitrary", "arbitrary")),
        name="gdn_prompt",
    )(qkv, qkv, qkv, conv_w, conv_w, conv_w, z, gates, norm_w.reshape(DEPTH, 1, HEAD_DIM), _inv_masks())


def _gdn_sample_prep_body(x_ref, c_ref, w_ref, o_ref, *, normalize):
    w = w_ref[...]
    y = c_ref[0] * w[0:1]
    for j in range(1, CONV_W - 1):
        y = y + c_ref[j] * w[j:j + 1]
    y = y + x_ref[...] * w[CONV_W - 1:CONV_W]
    y = _silu(y)
    if normalize:
        is_q = pl.program_id(0) < QK_W // y.shape[1]
        scale = jnp.where(is_q, HEAD_DIM ** -0.5, 1.0)
        for h in range(y.shape[1] // HEAD_DIM):
            yh = y[:, h * HEAD_DIM:(h + 1) * HEAD_DIM]
            yh = yh * lax.rsqrt(jnp.sum(yh * yh, axis=-1, keepdims=True) + EPS) * scale
            o_ref[:, h * HEAD_DIM:(h + 1) * HEAD_DIM] = yh
    else:
        o_ref[...] = y


def _gdn_sample_prep(qkv, cache_t, conv_w, *, layer, col0, n, normalize):
    tn = 1024
    cb0 = col0 // tn
    rb = N_PROMPT // DEC_BATCH
    return pl.pallas_call(
        functools.partial(_gdn_sample_prep_body, normalize=normalize),
        grid=(n // tn,),
        in_specs=[
            pl.BlockSpec((DEC_BATCH, tn), lambda j: (rb, cb0 + j)),
            pl.BlockSpec((None, CONV_W - 1, DEC_BATCH, tn), lambda j: (layer, 0, 0, cb0 + j)),
            pl.BlockSpec((None, CONV_W, tn), lambda j: (layer, 0, cb0 + j)),
        ],
        out_specs=pl.BlockSpec((DEC_BATCH, tn), lambda j: (0, j)),
        out_shape=jax.ShapeDtypeStruct((DEC_BATCH, n), F32),
        compiler_params=_cparams(("arbitrary",)),
        name="gdn_sample_prep",
    )(qkv, cache_t, conv_w)


def _gdn_sample_body(s_ref, qt_ref, kt_ref, v_ref, z_ref, eg_ref, beta_ref, nw_ref, so_ref, o_ref):
    qt = qt_ref[...]
    kt = kt_ref[...]
    v = v_ref[...]
    z = z_ref[...]
    egs = eg_ref[...]
    betas = beta_ref[...]
    nw = nw_ref[...]
    for h in range(V_HEADS):
        hq = h // 2
        kc = kt[:, hq:hq + 1]
        qc = qt[:, hq:hq + 1]
        s = s_ref[h]
        eg = egs[:, h:h + 1]
        beta = betas[:, h:h + 1]
        ks = jnp.sum(s * kc, axis=0, keepdims=True)
        qs = jnp.sum(s * qc, axis=0, keepdims=True)
        vh = v[:, h * HEAD_DIM:(h + 1) * HEAD_DIM]
        v_new = beta * vh - (beta * eg) * ks
        qk = jnp.sum(qc * kc, axis=0, keepdims=True)
        o = eg * qs + qk * v_new
        so_ref[h] = s * eg + kc * v_new
        zz = z[:, h * HEAD_DIM:(h + 1) * HEAD_DIM]
        on = o * lax.rsqrt(jnp.mean(o * o, axis=-1, keepdims=True) + EPS) * nw * _silu(zz)
        o_ref[:, h * HEAD_DIM:(h + 1) * HEAD_DIM] = on.astype(o_ref.dtype)


def _gdn_sample(state, qt, kt, v, z, eg, beta, norm_w, *, layer):
    r3 = lambda w: pl.BlockSpec((None, 1, w), lambda b: (b, 0, 0))
    return pl.pallas_call(
        _gdn_sample_body,
        grid=(DEC_BATCH,),
        in_specs=[
            pl.BlockSpec((None, None, V_HEADS, HEAD_DIM, HEAD_DIM), lambda b: (layer, b, 0, 0, 0)),
            pl.BlockSpec((None, HEAD_DIM, QK_HEADS), lambda b: (b, 0, 0)),
            pl.BlockSpec((None, HEAD_DIM, QK_HEADS), lambda b: (b, 0, 0)),
            r3(V_W), r3(V_W), r3(V_HEADS), r3(V_HEADS),
            pl.BlockSpec((None, 1, HEAD_DIM), lambda b: (layer, 0, 0)),
        ],
        out_specs=[
            pl.BlockSpec((None, V_HEADS, HEAD_DIM, HEAD_DIM), lambda b: (b, 0, 0, 0)),
            r3(V_W),
        ],
        out_shape=[jax.ShapeDtypeStruct((DEC_BATCH, V_HEADS, HEAD_DIM, HEAD_DIM), F32),
                   jax.ShapeDtypeStruct((DEC_BATCH, 1, V_W), BF16)],
        compiler_params=_cparams(("arbitrary",)),
        name="gdn_sample",
    )(state, qt, kt, v, z, eg, beta, norm_w.reshape(DEPTH, 1, HEAD_DIM))


def _s5_in_proj(ub, wb_ref, re_ref, im_ref):
    for j in range(S5_NBLK):
        r = jnp.dot(ub[:, j * LANES:(j + 1) * LANES], wb_ref[j], preferred_element_type=F32)
        re_ref[:, j * S5_BS:(j + 1) * S5_BS] = r[:, :S5_BS]
        im_ref[:, j * S5_BS:(j + 1) * S5_BS] = r[:, S5_BS:]


def _s5_out_proj(u, re_ref, im_ref, wc_ref, d_ref, y_ref):
    for j in range(S5_NBLK):
        xr = re_ref[:, j * S5_BS:(j + 1) * S5_BS].astype(BF16)
        xi = im_ref[:, j * S5_BS:(j + 1) * S5_BS].astype(BF16)
        y = jnp.dot(xr, wc_ref[j, :S5_BS], preferred_element_type=F32)
        y = y + jnp.dot(xi, wc_ref[j, S5_BS:], preferred_element_type=F32)
        y = y + d_ref[:, j * LANES:(j + 1) * LANES] * u[:, j * LANES:(j + 1) * LANES]
        y_ref[:, j * LANES:(j + 1) * LANES] = _gelu_tanh(y)


def _s5_prompt_body(*refs):
    u_refs = refs[:BATCH]
    wb_ref, wc_ref, lr_ref, li_ref, d_ref, y_ref, sre_ref, sim_ref, re_ref, im_ref, utm_ref, ytm_ref = refs[BATCH:]

    @pl.when(pl.program_id(0) == 0)
    def _():
        sre_ref[...] = jnp.zeros_like(sre_ref)
        sim_ref[...] = jnp.zeros_like(sim_ref)
        utm_ref[...] = jnp.zeros_like(utm_ref)

    for b in range(BATCH):
        for j in range(S5_NBLK):
            utm_ref[j, pl.ds(b, S5_TT, stride=SUBLANES), :] = u_refs[b][:, j * LANES:(j + 1) * LANES]
    for j in range(S5_NBLK):
        r = jnp.dot(utm_ref[j].astype(BF16), wb_ref[j], preferred_element_type=F32)
        re_ref[:, j * S5_BS:(j + 1) * S5_BS] = r[:, :S5_BS]
        im_ref[:, j * S5_BS:(j + 1) * S5_BS] = r[:, S5_BS:]
    cw = 1024
    for cb in range(S5_STATES // cw):
        cols = slice(cb * cw, (cb + 1) * cw)
        lr = lr_ref[:, cols]
        li = li_ref[:, cols]

        def step(t, carry):
            xr, xi = carry
            rows = pl.ds(pl.multiple_of(t * SUBLANES, SUBLANES), SUBLANES)
            nr = lr * xr - li * xi + re_ref[rows, cols]
            ni = lr * xi + li * xr + im_ref[rows, cols]
            re_ref[rows, cols] = nr
            im_ref[rows, cols] = ni
            return nr, ni

        xr, xi = lax.fori_loop(0, S5_TT, step, (sre_ref[:, cols], sim_ref[:, cols]), unroll=2)
        sre_ref[:, cols] = xr
        sim_ref[:, cols] = xi
    for j in range(S5_NBLK):
        xr = re_ref[:, j * S5_BS:(j + 1) * S5_BS].astype(BF16)
        xi = im_ref[:, j * S5_BS:(j + 1) * S5_BS].astype(BF16)
        y = jnp.dot(xr, wc_ref[j, :S5_BS], preferred_element_type=F32)
        y = y + jnp.dot(xi, wc_ref[j, S5_BS:], preferred_element_type=F32)
        y = y + d_ref[:, j * LANES:(j + 1) * LANES] * utm_ref[j]
        ytm_ref[j] = _gelu_tanh(y)
    for b in range(BATCH):
        for j in range(S5_NBLK):
            y_ref[b, :, j * LANES:(j + 1) * LANES] = ytm_ref[j, pl.ds(b, S5_TT, stride=SUBLANES), :]


def _s5_prompt(tail, wb, wc, lam_re8, lam_im8, d_row, *, layer):
    rows = S5_TT * SUBLANES
    steps = SEQ // S5_TT
    full = lambda shape: pl.BlockSpec(shape, lambda t: (0,) * len(shape))
    lay = lambda shape: pl.BlockSpec((None,) + shape, lambda t: (layer,) + (0,) * len(shape))
    return pl.pallas_call(
        _s5_prompt_body,
        grid=(steps,),
        in_specs=[pl.BlockSpec((S5_TT, D_MODEL), functools.partial(lambda b, t: (b * steps + t, 0), b))
                  for b in range(BATCH)] + [
            lay((S5_NBLK, LANES, 2 * S5_BS)),
            lay((S5_NBLK, 2 * S5_BS, LANES)),
            lay((SUBLANES, S5_STATES)),
            lay((SUBLANES, S5_STATES)),
            lay((1, D_MODEL)),
        ],
        out_specs=[
            pl.BlockSpec((BATCH, S5_TT, D_MODEL), lambda t: (0, t, 0)),
            full((SUBLANES, S5_STATES)),
            full((SUBLANES, S5_STATES)),
        ],
        out_shape=[jax.ShapeDtypeStruct((BATCH, SEQ, D_MODEL), F32),
                   jax.ShapeDtypeStruct((SUBLANES, S5_STATES), F32),
                   jax.ShapeDtypeStruct((SUBLANES, S5_STATES), F32)],
        scratch_shapes=[pltpu.VMEM((rows, S5_STATES), F32), pltpu.VMEM((rows, S5_STATES), F32),
                        pltpu.VMEM((S5_NBLK, rows, LANES), F32), pltpu.VMEM((S5_NBLK, rows, LANES), F32)],
        compiler_params=_cparams(("arbitrary",)),
        name="s5_prompt",
    )(*([tail] * BATCH), wb, wc, lam_re8, lam_im8, d_row)


def _s5_sample_body(u_ref, x0r_ref, x0i_ref, wb_ref, wc_ref, lr_ref, li_ref, d_ref, y_ref, x1r_ref, x1i_ref):
    u = u_ref[...]
    _s5_in_proj(u.astype(BF16), wb_ref, x1r_ref, x1i_ref)
    lr = lr_ref[...]
    li = li_ref[...]
    xr = x0r_ref[...]
    xi = x0i_ref[...]
    x1r_ref[...] = x1r_ref[...] + (lr * xr - li * xi)
    x1i_ref[...] = x1i_ref[...] + (lr * xi + li * xr)
    _s5_out_proj(u, x1r_ref, x1i_ref, wc_ref, d_ref, y_ref)


def _s5_sample(u_all, x0r, x0i, wb, wc, lam_re, lam_im, d_row, *, layer):
    full = lambda shape: pl.BlockSpec(shape, lambda i: (0,) * len(shape))
    lay = lambda shape: pl.BlockSpec((None,) + shape, lambda i: (layer,) + (0,) * len(shape))
    st = lay((DEC_BATCH, S5_STATES))
    return pl.pallas_call(
        _s5_sample_body,
        grid=(1,),
        in_specs=[
            pl.BlockSpec((DEC_BATCH, D_MODEL), lambda i: (N_PROMPT // DEC_BATCH, 0)),
            st, st,
            lay((S5_NBLK, LANES, 2 * S5_BS)),
            lay((S5_NBLK, 2 * S5_BS, LANES)),
            lay((1, S5_STATES)), lay((1, S5_STATES)), lay((1, D_MODEL)),
        ],
        out_specs=[full((DEC_BATCH, D_MODEL)), full((DEC_BATCH, S5_STATES)), full((DEC_BATCH, S5_STATES))],
        out_shape=[jax.ShapeDtypeStruct((DEC_BATCH, D_MODEL), F32),
                   jax.ShapeDtypeStruct((DEC_BATCH, S5_STATES), F32),
                   jax.ShapeDtypeStruct((DEC_BATCH, S5_STATES), F32)],
        compiler_params=_cparams(("arbitrary",)),
        name="s5_sample",
    )(u_all, x0r, x0i, wb, wc, lam_re, lam_im, d_row)


def _s5_params(lam_re, lam_im, log_dt, b_re, b_im, c_re, c_im, d_skip):
    lam = lax.complex(lam_re, lam_im)
    dt = jnp.exp(log_dt)[:, None]
    lam_bar = jnp.exp(lam * dt)
    b_bar = ((lam_bar - 1.0) / lam)[..., None] * lax.complex(b_re, b_im)
    eye = jnp.eye(S5_BLK, dtype=F32)

    def in_blocks(b):
        b = b.reshape(S5_NBLK, S5_BLK, S5_N, S5_P)
        return jnp.einsum("jgnp,gh->jgphn", b, eye).reshape(S5_NBLK, S5_BLK * S5_P, S5_BS)

    def out_blocks(c):
        c = c.reshape(S5_NBLK, S5_BLK, S5_P, S5_N)
        return jnp.einsum("jgpn,gh->jgnhp", c, eye).reshape(S5_NBLK, S5_BS, S5_BLK * S5_P)

    wb = jnp.concatenate([in_blocks(b_bar.real), in_blocks(b_bar.imag)], axis=2).astype(BF16)
    wc = jnp.concatenate([out_blocks(c_re), -out_blocks(c_im)], axis=1).astype(BF16)
    return (wb, wc, lam_bar.real.reshape(1, S5_STATES), lam_bar.imag.reshape(1, S5_STATES),
            d_skip.reshape(1, D_MODEL))


def _router_body(h_ref, w_ref, b_ref, idx_ref, wt_ref):
    logits = jnp.dot(h_ref[...].astype(BF16), w_ref[...].astype(BF16), preferred_element_type=F32)
    scores = _sigmoid(logits)
    biased = scores + b_ref[...]
    lane = lax.broadcasted_iota(jnp.int32, biased.shape, 1)
    kcol = lax.broadcasted_iota(jnp.int32, (biased.shape[0], TOP_K), 1)
    idx = jnp.zeros((biased.shape[0], TOP_K), jnp.int32)
    sel = jnp.zeros((biased.shape[0], TOP_K), F32)
    for k in range(TOP_K):
        m = jnp.max(biased, axis=-1, keepdims=True)
        first = jnp.min(jnp.where(biased == m, lane, N_EXPERTS), axis=-1, keepdims=True)
        hit = lane == first
        s = jnp.sum(jnp.where(hit, scores, 0.0), axis=-1, keepdims=True)
        idx = jnp.where(kcol == k, first, idx)
        sel = jnp.where(kcol == k, s, sel)
        biased = jnp.where(hit, -jnp.inf, biased)
    idx_ref[...] = idx
    wt_ref[...] = sel / jnp.sum(sel, axis=-1, keepdims=True) * ROUTED_SCALE


def _router(h, w_router, router_bias, *, layer):
    tm = MM_TM
    return pl.pallas_call(
        _router_body,
        grid=(M_ROWS // tm,),
        in_specs=[
            pl.BlockSpec((tm, D_MODEL), lambda i: (i, 0)),
            pl.BlockSpec((None, D_MODEL, N_EXPERTS), lambda i: (layer, 0, 0)),
            pl.BlockSpec((None, 1, N_EXPERTS), lambda i: (layer, 0, 0)),
        ],
        out_specs=[pl.BlockSpec((tm, TOP_K), lambda i: (i, 0))] * 2,
        out_shape=[jax.ShapeDtypeStruct((M_ROWS, TOP_K), jnp.int32),
                   jax.ShapeDtypeStruct((M_ROWS, TOP_K), F32)],
        compiler_params=_cparams(("arbitrary",)),
        name="router",
    )(h, w_router, router_bias.reshape(DEPTH, 1, N_EXPERTS))


def _routing_tables(top_idx):
    flat_e = top_idx.reshape(-1)
    order = jnp.argsort(flat_e).astype(jnp.int32)
    experts = jnp.arange(N_EXPERTS, dtype=jnp.int32)
    sizes = jnp.sum((flat_e[:, None] == experts[None, :]).astype(jnp.int32), axis=0)
    starts = jnp.cumsum(sizes) - sizes
    padded = (sizes + MOE_BS - 1) // MOE_BS * MOE_BS
    pad_ends = jnp.cumsum(padded)
    pad_starts = pad_ends - padded
    i = jnp.arange(N_ASSIGN, dtype=jnp.int32)[:, None]
    mine = jnp.logical_and(i >= starts[None, :], i < (starts + sizes)[None, :])
    dest = i[:, 0] + jnp.sum(jnp.where(mine, (pad_starts - starts)[None, :], 0), axis=1)
    _, pos = lax.sort((order, dest.astype(jnp.int32)), num_keys=1)
    blk_start = jnp.arange(MOE_BLOCKS, dtype=jnp.int32) * MOE_BS
    block_e = jnp.minimum(jnp.sum(blk_start[:, None] >= pad_ends[None, :], axis=1), N_EXPERTS - 1).astype(jnp.int32)
    sorted_tok = jnp.concatenate([order // TOP_K, jnp.zeros((MOE_BS,), jnp.int32)])
    off = jnp.clip(starts[block_e] + blk_start - pad_starts[block_e], 0, N_ASSIGN)
    return sorted_tok, off.astype(jnp.int32), pos, block_e


def _to_slabs(x):
    return x.reshape(x.shape[0] * ROW_SLAB, LANES)


def _load_slab_rows(ref, n):
    return jnp.concatenate([ref[pl.ds(j, n, stride=SLAB_PITCH), :] for j in range(ROW_SLAB)], axis=1)


def _expert_gather(stok_ref, base, h_hbm, xbuf, sem, slot):
    for r in range(MOE_BS):
        src = h_hbm.at[pl.ds(pl.multiple_of(stok_ref[base + r] * ROW_SLAB, ROW_SLAB), ROW_SLAB), :]
        pltpu.make_async_copy(src, xbuf.at[slot, pl.ds(r * SLAB_PITCH, ROW_SLAB), :], sem.at[slot]).start()


def _expert_gather_wait(h_hbm, xbuf, sem, slot):
    pltpu.make_async_copy(h_hbm.at[pl.ds(0, MOE_BS * ROW_SLAB), :], xbuf.at[slot, pl.ds(0, MOE_BS * ROW_SLAB), :],
                          sem.at[slot]).wait()


def _expert_weight_copies(layer, e, wslot, w_hbm, wbuf, wsem):
    return [pltpu.make_async_copy(w.at[layer, e], buf.at[wslot], wsem.at[wslot]) for w, buf in zip(w_hbm, wbuf)]


def _experts_body(be_ref, first_ref, nxt_ref, wslot_ref, off_ref, stok_ref, h_hbm, wg_hbm, wu_hbm, wd_hbm, o_ref,
                  xbuf, sem, wgf, wuf, wdf, wsem, wgb, wub, wdb, *, layer):
    b = pl.program_id(0)
    nb = pl.num_programs(0)
    slot = lax.rem(b, 2)
    w_hbm = (wg_hbm, wu_hbm, wd_hbm)
    wbuf = (wgf, wuf, wdf)
    ws = wslot_ref[b]

    @pl.when(b == 0)
    def _():
        _expert_gather(stok_ref, off_ref[0], h_hbm, xbuf, sem, 0)
        for c in _expert_weight_copies(layer, be_ref[0], 0, w_hbm, wbuf, wsem):
            c.start(priority=1)

    @pl.when(first_ref[b] == 1)
    def _():
        for c in _expert_weight_copies(layer, be_ref[b], ws, w_hbm, wbuf, wsem):
            c.wait()

        @pl.when(nxt_ref[b] >= 0)
        def _():
            for c in _expert_weight_copies(layer, nxt_ref[b], 1 - ws, w_hbm, wbuf, wsem):
                c.start(priority=1)

        wgb[...] = wgf[ws].astype(BF16)
        wub[...] = wuf[ws].astype(BF16)
        wdb[...] = wdf[ws].astype(BF16)

    _expert_gather_wait(h_hbm, xbuf, sem, slot)
    _expert_gather(stok_ref, off_ref[jnp.minimum(b + 1, nb - 1)], h_hbm, xbuf, sem, 1 - slot)
    x = _load_slab_rows(xbuf.at[slot], MOE_BS).astype(BF16)
    g = jnp.dot(x, wgb[...], preferred_element_type=F32)
    u = jnp.dot(x, wub[...], preferred_element_type=F32)
    a = (_silu(g) * u).astype(BF16)
    y = jnp.dot(a, wdb[...], preferred_element_type=F32)
    for j in range(ROW_SLAB):
        o_ref[pl.ds(j, MOE_BS, stride=ROW_SLAB), :] = y[:, j * LANES:(j + 1) * LANES]

    @pl.when(b == nb - 1)
    def _():
        _expert_gather_wait(h_hbm, xbuf, sem, 1 - slot)


def _experts(h_slabs, sorted_tok, off, block_e, w_gate, w_up, w_down, *, layer):
    prev_e = jnp.concatenate([jnp.full((1,), -1, jnp.int32), block_e[:-1]])
    first = (block_e != prev_e).astype(jnp.int32)
    run = jnp.cumsum(first) - 1
    blk = jnp.arange(MOE_BLOCKS, dtype=jnp.int32)
    later_first = jnp.logical_and(first[None, :] == 1, blk[None, :] > blk[:, None])
    nxt_blk = jnp.min(jnp.where(later_first, blk[None, :], MOE_BLOCKS), axis=1)
    nxt = jnp.where(nxt_blk < MOE_BLOCKS, block_e[jnp.minimum(nxt_blk, MOE_BLOCKS - 1)], -1).astype(jnp.int32)
    wslot = (run % 2).astype(jnp.int32)

    any_spec = pl.BlockSpec(memory_space=pl.ANY)
    grid_spec = pltpu.PrefetchScalarGridSpec(
        num_scalar_prefetch=6,
        grid=(MOE_BLOCKS,),
        in_specs=[any_spec, any_spec, any_spec, any_spec],
        out_specs=pl.BlockSpec((MOE_BS * ROW_SLAB, LANES), lambda b, *_: (b, 0)),
        scratch_shapes=[
            pltpu.VMEM((2, MOE_BS * SLAB_PITCH, LANES), F32),
            pltpu.SemaphoreType.DMA((2,)),
            pltpu.VMEM((2, D_MODEL, D_EXPERT), F32),
            pltpu.VMEM((2, D_MODEL, D_EXPERT), F32),
            pltpu.VMEM((2, D_EXPERT, D_MODEL), F32),
            pltpu.SemaphoreType.DMA((2,)),
            pltpu.VMEM((D_MODEL, D_EXPERT), BF16),
            pltpu.VMEM((D_MODEL, D_EXPERT), BF16),
            pltpu.VMEM((D_EXPERT, D_MODEL), BF16),
        ],
    )
    return pl.pallas_call(
        functools.partial(_experts_body, layer=layer),
        grid_spec=grid_spec,
        out_shape=jax.ShapeDtypeStruct((MOE_ROWS * ROW_SLAB, LANES), F32),
        compiler_params=_cparams(("arbitrary",)),
        name="experts",
    )(block_e, first, nxt, wslot, off, sorted_tok, h_slabs, w_gate, w_up, w_down)


def _shared_body(h_ref, wg_ref, wu_ref, wd_ref, o_ref, wgb, wub, wdb):
    @pl.when(pl.program_id(0) == 0)
    def _():
        wgb[...] = wg_ref[...].astype(BF16)
        wub[...] = wu_ref[...].astype(BF16)
        wdb[...] = wd_ref[...].astype(BF16)

    x = h_ref[...].astype(BF16)
    g = jnp.dot(x, wgb[...], preferred_element_type=F32)
    u = jnp.dot(x, wub[...], preferred_element_type=F32)
    a = (_silu(g) * u).astype(BF16)
    o_ref[...] = jnp.dot(a, wdb[...], preferred_element_type=F32)


def _shared_expert(h, w_gate, w_up, w_down, *, layer):
    tm = MM_TM
    return pl.pallas_call(
        _shared_body,
        grid=(M_ROWS // tm,),
        in_specs=[
            pl.BlockSpec((tm, D_MODEL), lambda i: (i, 0)),
            pl.BlockSpec((None, D_MODEL, D_EXPERT), lambda i: (layer, 0, 0)),
            pl.BlockSpec((None, D_MODEL, D_EXPERT), lambda i: (layer, 0, 0)),
            pl.BlockSpec((None, D_EXPERT, D_MODEL), lambda i: (layer, 0, 0)),
        ],
        out_specs=pl.BlockSpec((tm, D_MODEL), lambda i: (i, 0)),
        out_shape=jax.ShapeDtypeStruct((M_ROWS, D_MODEL), F32),
        scratch_shapes=[pltpu.VMEM((D_MODEL, D_EXPERT), BF16), pltpu.VMEM((D_MODEL, D_EXPERT), BF16),
                        pltpu.VMEM((D_EXPERT, D_MODEL), BF16)],
        compiler_params=_cparams(("arbitrary",)),
        name="shared_expert",
    )(h, w_gate, w_up, w_down)


def _combine_gather(pos_ref, y_hbm, buf, sem, slot):
    def issue(t, carry):
        for k in range(TOP_K):
            p = pos_ref[0, t * TOP_K + k]
            src = y_hbm.at[pl.ds(pl.multiple_of(p * ROW_SLAB, ROW_SLAB), ROW_SLAB), :]
            dst = buf.at[slot, k, pl.ds(t * SLAB_PITCH, ROW_SLAB), :]
            pltpu.make_async_copy(src, dst, sem.at[slot]).start(priority=k % 2)
        return carry

    lax.fori_loop(0, COMB_T, issue, 0, unroll=2)


def _combine_gather_wait(y_hbm, buf, sem, slot):
    for k in range(TOP_K):
        pltpu.make_async_copy(y_hbm.at[pl.ds(0, COMB_T * ROW_SLAB), :],
                              buf.at[slot, k, pl.ds(0, COMB_T * ROW_SLAB), :], sem.at[slot]).wait()


def _combine_body(pos_ref, posn_ref, y_hbm, wt_ref, sh_ref, o_ref, buf, sem):
    i = pl.program_id(0)
    slot = lax.rem(i, 2)

    @pl.when(i == 0)
    def _():
        _combine_gather(pos_ref, y_hbm, buf, sem, 0)

    @pl.when(i + 1 < pl.num_programs(0))
    def _():
        _combine_gather(posn_ref, y_hbm, buf, sem, 1 - slot)

    _combine_gather_wait(y_hbm, buf, sem, slot)
    wt = wt_ref[...]
    wk = [jnp.broadcast_to(wt[:, k:k + 1], (COMB_T, LANES)) for k in range(TOP_K)]
    for j in range(ROW_SLAB):
        cols = slice(j * LANES, (j + 1) * LANES)
        acc = sh_ref[:, cols]
        for k in range(TOP_K):
            acc = acc + buf[slot, k, pl.ds(j, COMB_T, stride=SLAB_PITCH), :] * wk[k]
        o_ref[:, cols] = acc


def _combine(y_slabs, pos, top_w, shared):
    nt = M_ROWS // COMB_T
    pos3 = pos.reshape(nt, 1, COMB_T * TOP_K)
    smem_pos = lambda f: pl.BlockSpec((None, 1, COMB_T * TOP_K), f, memory_space=pltpu.SMEM)
    return pl.pallas_call(
        _combine_body,
        grid=(nt,),
        in_specs=[
            smem_pos(lambda i: (i, 0, 0)),
            smem_pos(lambda i: (jnp.minimum(i + 1, nt - 1), 0, 0)),
            pl.BlockSpec(memory_space=pl.ANY),
            pl.BlockSpec((COMB_T, TOP_K), lambda i: (i, 0)),
            pl.BlockSpec((COMB_T, D_MODEL), lambda i: (i, 0)),
        ],
        out_specs=pl.BlockSpec((COMB_T, D_MODEL), lambda i: (i, 0)),
        out_shape=jax.ShapeDtypeStruct((M_ROWS, D_MODEL), F32),
        scratch_shapes=[pltpu.VMEM((2, TOP_K, COMB_T * SLAB_PITCH, LANES), F32), pltpu.SemaphoreType.DMA((2,))],
        compiler_params=_cparams(("arbitrary",)),
        name="combine",
    )(pos3, pos3, y_slabs, top_w, shared)


def _mixer(l, h1, p, cache_t, state_gdn, s5_x0r, s5_x0i):
    w_in = p["w_in"]
    qkv = _mm(h1, w_in, layer=l, col0=0, n=QKV_W, tn=1024, name="in_qkv")
    z = _mm(h1, w_in, layer=l, col0=COL_Z, n=V_W, tn=1024, name="in_z")
    ba = _mm(h1, p["w_ba"], layer=l, col0=0, n=LANES, tn=LANES, name="in_ba")
    tail = _mm(h1, p["w_tail"], layer=l, col0=0, n=3 * D_MODEL, tn=1024, name="in_tail")

    gb, gcum = _gates(ba, p["alog_row"][l], p["dtb_row"][l])
    o_p, gdn_p = _gdn_prompt(qkv, z, p["conv_w"], _gdn_gate_rows(gb, gcum), p["gdn_norm_w"], layer=l)

    qk_s = _gdn_sample_prep(qkv, cache_t, p["conv_w"], layer=l, col0=0, n=2 * QK_W, normalize=True)
    v_s = _gdn_sample_prep(qkv, cache_t, p["conv_w"], layer=l, col0=2 * QK_W, n=V_W, normalize=False)
    col_form = lambda x: x.reshape(DEC_BATCH, QK_HEADS, HEAD_DIM).transpose(0, 2, 1)
    qt = col_form(qk_s[:, :QK_W])
    kt = col_form(qk_s[:, QK_W:])
    beta_s = gb[N_PROMPT:, :V_HEADS].reshape(DEC_BATCH, 1, V_HEADS)
    eg_s = jnp.exp(gb[N_PROMPT:, V_HEADS:2 * V_HEADS]).reshape(DEC_BATCH, 1, V_HEADS)
    z_s = z[N_PROMPT:].reshape(DEC_BATCH, 1, V_W)
    gdn_s, o_s = _gdn_sample(state_gdn, qt, kt, v_s.reshape(DEC_BATCH, 1, V_W), z_s, eg_s, beta_s,
                             p["gdn_norm_w"], layer=l)
    o_all = jnp.concatenate([o_p, o_s.reshape(DEC_BATCH, V_W)], axis=0)
    y_a = _mm(o_all, p["w_br_a"], layer=l, col0=0, n=D_MODEL, tn=512, name="br_a")

    wb, wc, lam_re, lam_im, d_row = p["s5"]
    lam_re8 = jnp.broadcast_to(lam_re, (DEPTH, SUBLANES, S5_STATES))
    lam_im8 = jnp.broadcast_to(lam_im, (DEPTH, SUBLANES, S5_STATES))
    y_p, re_p, im_p = _s5_prompt(tail, wb, wc, lam_re8, lam_im8, d_row, layer=l)
    y_s, re_s, im_s = _s5_sample(tail, s5_x0r, s5_x0i, wb, wc, lam_re, lam_im, d_row, layer=l)
    y_all = jnp.concatenate([y_p.reshape(N_PROMPT, D_MODEL), y_s], axis=0)
    ys_spec = pl.BlockSpec((MM_TM, 1024), lambda j, i: (i, j))
    y_glu = _mm(y_all, p["w_glu"], layer=l, col0=0, n=D_MODEL, tn=1024, out_dtype=BF16, epilogue="glu",
                extra=(y_all,), extra_specs=(ys_spec,), name="glu")

    tn = 1024
    nb = D_MODEL // tn
    merged = _mm(y_glu, p["w_br_b"], layer=l, col0=0, n=D_MODEL, tn=tn, out_dtype=BF16, epilogue="merge",
                 extra=(y_a, tail, tail),
                 extra_specs=(pl.BlockSpec((MM_TM, tn), lambda j, i: (i, j)),
                              pl.BlockSpec((MM_TM, tn), lambda j, i: (i, nb + j)),
                              pl.BlockSpec((MM_TM, tn), lambda j, i: (i, 2 * nb + j))),
                 name="br_b_merge")
    mix = _mm(merged, p["w_out"], layer=l, col0=0, n=D_MODEL, tn=1024, name="out_proj")

    conv_p = jnp.stack([qkv[(b + 1) * SEQ - (CONV_W - 1):(b + 1) * SEQ] for b in range(BATCH)])
    conv_s = jnp.concatenate([cache_t[l, 1:].transpose(1, 0, 2), qkv[N_PROMPT:, None, :]], axis=1)
    states = (conv_p, gdn_p,
              re_p[:BATCH].reshape(BATCH, S5_G, S5_N), im_p[:BATCH].reshape(BATCH, S5_G, S5_N),
              conv_s, gdn_s,
              re_s.reshape(DEC_BATCH, S5_G, S5_N), im_s.reshape(DEC_BATCH, S5_G, S5_N))
    return mix, states


def _moe(l, h2, p):
    top_idx, top_w = _router(h2, p["w_router"], p["router_bias"], layer=l)
    sorted_tok, off, pos, block_e = _routing_tables(top_idx)
    y_slabs = _experts(_to_slabs(h2), sorted_tok, off, block_e, p["w_exp_gate"], p["w_exp_up"], p["w_exp_down"],
                       layer=l)
    shared = _shared_expert(h2, p["w_sh_gate"], p["w_sh_up"], p["w_sh_down"], layer=l)
    return _combine(y_slabs, pos, top_w, shared)


def kernel(x_prompt, x_sample, cache_conv, state_gdn, state_s5_re, state_s5_im, c_prompt, c_sample, ln_in_g, ln_in_b, w_ada, b_ada, w_in, conv_w, gdn_a_log, gdn_dt_bias, gdn_norm_w, s5_lam_re, s5_lam_im, s5_log_dt, s5_b_re, s5_b_im, s5_c_re, s5_c_im, s5_d, w_glu, w_br_a, w_br_b, w_out, ln1_g, ln1_b, w_router, router_bias, w_exp_gate, w_exp_up, w_exp_down, w_sh_gate, w_sh_up, w_sh_down, ln2_g, ln2_b):
    c_all = jnp.concatenate([c_prompt, jnp.zeros((SUBLANES - BATCH, D_MODEL), F32), c_sample], axis=0)
    mod = _ada_all(c_all, w_ada, b_ada)
    mod_p = mod[:, :SUBLANES]
    mod_s = mod[:, SUBLANES:]

    lane_pad = lambda a: jnp.pad(a, ((0, 0), (V_HEADS, LANES - 2 * V_HEADS)))[:, None, :]
    p = {
        "w_in": w_in,
        "w_ba": jnp.pad(w_in[:, :, COL_BA:COL_TAIL], ((0, 0), (0, 0), (0, LANES - 2 * V_HEADS))),
        "w_tail": w_in[:, :, COL_TAIL:],
        "alog_row": lane_pad(gdn_a_log), "dtb_row": lane_pad(gdn_dt_bias),
        "conv_w": conv_w, "gdn_norm_w": gdn_norm_w,
        "s5": jax.vmap(_s5_params)(s5_lam_re, s5_lam_im, s5_log_dt, s5_b_re, s5_b_im, s5_c_re, s5_c_im, s5_d),
        "w_glu": w_glu, "w_br_a": w_br_a, "w_br_b": w_br_b, "w_out": w_out,
        "w_router": w_router, "router_bias": router_bias,
        "w_exp_gate": w_exp_gate, "w_exp_up": w_exp_up, "w_exp_down": w_exp_down,
        "w_sh_gate": w_sh_gate, "w_sh_up": w_sh_up, "w_sh_down": w_sh_down,
    }
    cache_t = cache_conv.transpose(0, 2, 1, 3)
    s5_x0r = state_s5_re.reshape(DEPTH, DEC_BATCH, S5_STATES)
    s5_x0i = state_s5_im.reshape(DEPTH, DEC_BATCH, S5_STATES)

    x_all = jnp.concatenate([x_prompt.reshape(N_PROMPT, D_MODEL), x_sample.reshape(DEC_BATCH, D_MODEL)], axis=0)
    x, h1 = _ln_in(x_all, ln_in_g, ln_in_b, mod_p, mod_s)

    per_layer = []
    for l in range(DEPTH):
        mix, states = _mixer(l, h1, p, cache_t, state_gdn, s5_x0r, s5_x0i)
        per_layer.append(states)
        x, h2 = _deepnorm(x, mix, ln1_g, ln1_b, mod_p, mod_s, layer=l, gate_comp=2, mod_layer=l,
                          sc_comp=4, sh_comp=3, h_dtype=F32)
        ffn = _moe(l, h2, p)
        nxt = min(l + 1, DEPTH - 1)
        x, h1 = _deepnorm(x, ffn, ln2_g, ln2_b, mod_p, mod_s, layer=l, gate_comp=5, mod_layer=nxt,
                          sc_comp=1, sh_comp=0, h_dtype=BF16)

    stacked = [jnp.stack([s[i] for s in per_layer]) for i in range(8)]
    return (x[:N_PROMPT].reshape(BATCH, SEQ, D_MODEL), x[N_PROMPT:].reshape(DEC_BATCH, 1, D_MODEL), *stacked)
```

```python
import functools
import math

import jax
import jax.numpy as jnp
from jax import lax
from jax.experimental import pallas as pl
from jax.experimental.pallas import tpu as pltpu

F32 = jnp.float32
BF16 = jnp.bfloat16

D_MODEL = 2048
BATCH = 4
SEQ = 2048
DEPTH = 4
DEC_BATCH = 128
N_PROMPT = BATCH * SEQ
M_ROWS = N_PROMPT + DEC_BATCH

QK_HEADS = 16
V_HEADS = 32
HEAD_DIM = 128
QK_W = QK_HEADS * HEAD_DIM
V_W = V_HEADS * HEAD_DIM
QKV_W = 2 * QK_W + V_W
CONV_W = 4
CHUNK = 64
N_CHUNKS = SEQ // CHUNK

S5_P = 16
S5_N = 64
S5_G = D_MODEL // S5_P
S5_STATES = S5_G * S5_N

N_EXPERTS = 64
TOP_K = 8
D_EXPERT = D_MODEL // 4
ROUTED_SCALE = 2.5
N_ASSIGN = M_ROWS * TOP_K

ALPHA = (2 * DEPTH) ** 0.25
EPS = 1e-6

COL_Z = QKV_W
COL_BA = QKV_W + V_W
COL_TAIL = COL_BA + 2 * V_HEADS

LANES = 128
SUBLANES = 8
VMEM_LIMIT = 56 * 1024 * 1024

ROW_TILE = 128
N_ROW_TILES = M_ROWS // ROW_TILE
PROMPT_ROW_TILES = N_PROMPT // ROW_TILE
TILES_PER_SEQ = SEQ // ROW_TILE
MM_TM = 640

QUAD = 4 * CHUNK
GDN_QUADS = 4
GDN_PAIRS = 2 * GDN_QUADS
GDN_GROUPS = QK_HEADS // GDN_PAIRS
GDN_VH = 2 * GDN_PAIRS

S5_TT = 32
S5_BLK = 8
S5_NBLK = S5_G // S5_BLK
S5_BS = S5_BLK * S5_N

ROW_SLAB = D_MODEL // LANES
SLAB_PITCH = ROW_SLAB + 4
MOE_BS = 256
MOE_BLOCKS = -(-N_ASSIGN // MOE_BS) + N_EXPERTS
MOE_ROWS = MOE_BLOCKS * MOE_BS
COMB_T = 128


def _cparams(sem, vmem=VMEM_LIMIT):
    return pltpu.CompilerParams(dimension_semantics=sem, vmem_limit_bytes=vmem)


def _sigmoid(x):
    return jax.nn.sigmoid(x)


def _silu(x):
    return x * jax.nn.sigmoid(x)


def _gelu_tanh(x):
    c = math.sqrt(2.0 / math.pi)
    return 0.5 * x * (1.0 + jnp.tanh(c * (x + 0.044715 * (x * x * x))))


def _layer_norm(x, g, b):
    mu = jnp.mean(x, axis=-1, keepdims=True)
    xc = x - mu
    var = jnp.mean(xc * xc, axis=-1, keepdims=True)
    return xc * lax.rsqrt(var + EPS) * g + b


def _mm_body(*refs, prologue, epilogue, n_extra):
    a_ref, w_ref = refs[0], refs[1]
    extra = refs[2:2 + n_extra]
    o_ref = refs[2 + n_extra]
    wbf_ref = refs[3 + n_extra]

    @pl.when(pl.program_id(1) == 0)
    def _():
        wbf_ref[...] = w_ref[...].astype(BF16)

    a = a_ref[...]
    if prologue == "silu":
        a = _silu(a.astype(F32))
    acc = jnp.dot(a.astype(BF16), wbf_ref[...], preferred_element_type=F32)
    if epilogue == "bias":
        acc = acc + extra[0][...]
    elif epilogue == "glu":
        acc = extra[0][...].astype(F32) * _sigmoid(acc)
    elif epilogue == "merge":
        ya, ga, gb = extra[0][...], extra[1][...], extra[2][...]
        acc = _sigmoid(ga) * ya + _sigmoid(gb) * acc
    o_ref[...] = acc.astype(o_ref.dtype)


def _mm(a, w, *, layer, col0, n, tn, tm=MM_TM, out_dtype=F32, prologue=None, epilogue=None,
        extra=(), extra_specs=(), rows=None, name="mm"):
    rows = a.shape[0] if rows is None else rows
    k = a.shape[1]
    assert rows % tm == 0 and n % tn == 0 and col0 % tn == 0
    cb0 = col0 // tn
    grid = (n // tn, rows // tm)
    in_specs = [
        pl.BlockSpec((tm, k), lambda j, i: (i, 0)),
        pl.BlockSpec((None, k, tn), lambda j, i: (layer, 0, cb0 + j)),
    ] + list(extra_specs)
    return pl.pallas_call(
        functools.partial(_mm_body, prologue=prologue, epilogue=epilogue, n_extra=len(extra)),
        grid=grid,
        in_specs=in_specs,
        out_specs=pl.BlockSpec((tm, tn), lambda j, i: (i, j)),
        out_shape=jax.ShapeDtypeStruct((rows, n), out_dtype),
        scratch_shapes=[pltpu.VMEM((k, tn), BF16)],
        compiler_params=_cparams(("arbitrary", "arbitrary")),
        name=name,
    )(a, w, *extra)


def _ada_body(c_ref, w_ref, b_ref, o_ref):
    a = _silu(c_ref[...]).astype(BF16)
    o_ref[...] = jnp.dot(a, w_ref[...].astype(BF16), preferred_element_type=F32) + b_ref[...]


def _ada_all(c_all, w_ada, b_ada):
    rows = c_all.shape[0]
    tn = 1024
    nj = (6 * D_MODEL) // tn
    return pl.pallas_call(
        _ada_body,
        grid=(DEPTH * nj,),
        in_specs=[
            pl.BlockSpec((rows, D_MODEL), lambda g: (0, 0)),
            pl.BlockSpec((None, D_MODEL, tn), lambda g: (g // nj, 0, g % nj)),
            pl.BlockSpec((None, 1, tn), lambda g: (g // nj, 0, g % nj)),
        ],
        out_specs=pl.BlockSpec((None, rows, tn), lambda g: (g // nj, 0, g % nj)),
        out_shape=jax.ShapeDtypeStruct((DEPTH, rows, 6 * D_MODEL), F32),
        compiler_params=_cparams(("arbitrary",)),
        name="ada",
    )(c_all, w_ada, b_ada.reshape(DEPTH, 1, 6 * D_MODEL))


def _mod_pick(i, mp_ref, ms_ref):
    b = jnp.minimum(i // TILES_PER_SEQ, BATCH - 1)
    row = mp_ref[pl.ds(b, 1), :]
    return jnp.where(i < PROMPT_ROW_TILES, row, ms_ref[...])


def _mod_specs(layer, comp):
    return [
        pl.BlockSpec((None, SUBLANES, D_MODEL), lambda i: (layer, 0, comp)),
        pl.BlockSpec((None, DEC_BATCH, D_MODEL), lambda i: (layer, 0, comp)),
    ]


def _row_spec():
    return pl.BlockSpec((ROW_TILE, D_MODEL), lambda i: (i, 0))


def _vec_spec():
    return pl.BlockSpec((1, D_MODEL), lambda i: (0, 0))


def _ln_in_body(x_ref, g_ref, b_ref, scp, scs, shp, shs, x_out, h_out):
    i = pl.program_id(0)
    x = _layer_norm(x_ref[...], g_ref[...], b_ref[...])
    x_out[...] = x
    h = x * (1.0 + _mod_pick(i, scp, scs)) + _mod_pick(i, shp, shs)
    h_out[...] = h.astype(h_out.dtype)


def _ln_in(x_all, g, b, mod_p, mod_s):
    return pl.pallas_call(
        _ln_in_body,
        grid=(N_ROW_TILES,),
        in_specs=[_row_spec(), _vec_spec(), _vec_spec()] + _mod_specs(0, 1) + _mod_specs(0, 0),
        out_specs=[_row_spec(), _row_spec()],
        out_shape=[jax.ShapeDtypeStruct((M_ROWS, D_MODEL), F32),
                   jax.ShapeDtypeStruct((M_ROWS, D_MODEL), BF16)],
        compiler_params=_cparams(("arbitrary",)),
        name="ln_in",
    )(x_all, g.reshape(1, -1), b.reshape(1, -1), mod_p, mod_s, mod_p, mod_s)


def _deepnorm_body(x_ref, y_ref, g_ref, b_ref, gtp, gts, scp, scs, shp, shs, x_out, h_out):
    i = pl.program_id(0)
    r = ALPHA * x_ref[...] + _mod_pick(i, gtp, gts) * y_ref[...]
    x = _layer_norm(r, g_ref[...], b_ref[...])
    x_out[...] = x
    h = x * (1.0 + _mod_pick(i, scp, scs)) + _mod_pick(i, shp, shs)
    h_out[...] = h.astype(h_out.dtype)


def _deepnorm(x, y, g, b, mod_p, mod_s, *, layer, gate_comp, mod_layer, sc_comp, sh_comp, h_dtype):
    return pl.pallas_call(
        _deepnorm_body,
        grid=(N_ROW_TILES,),
        in_specs=[_row_spec(), _row_spec(),
                  pl.BlockSpec((None, 1, D_MODEL), lambda i: (layer, 0, 0)),
                  pl.BlockSpec((None, 1, D_MODEL), lambda i: (layer, 0, 0))]
        + _mod_specs(layer, gate_comp) + _mod_specs(mod_layer, sc_comp) + _mod_specs(mod_layer, sh_comp),
        out_specs=[_row_spec(), _row_spec()],
        out_shape=[jax.ShapeDtypeStruct((M_ROWS, D_MODEL), F32),
                   jax.ShapeDtypeStruct((M_ROWS, D_MODEL), h_dtype)],
        compiler_params=_cparams(("arbitrary",)),
        name="deepnorm",
    )(x, y, g.reshape(DEPTH, 1, D_MODEL), b.reshape(DEPTH, 1, D_MODEL),
      mod_p, mod_s, mod_p, mod_s, mod_p, mod_s)


def _gates_body(ba_ref, alog_ref, dtb_ref, gb_ref, gc_ref):
    x = ba_ref[...]
    lane = lax.broadcasted_iota(jnp.int32, x.shape, 1)
    beta = _sigmoid(x)
    z = x + dtb_ref[...]
    softplus = jnp.maximum(z, 0.0) + jnp.log1p(jnp.exp(-jnp.abs(z)))
    g = -jnp.exp(alog_ref[...]) * softplus
    is_beta = lane < V_HEADS
    is_g = jnp.logical_and(lane >= V_HEADS, lane < 2 * V_HEADS)
    g = jnp.where(is_g, g, 0.0)
    gb_ref[...] = jnp.where(is_beta, beta, g)
    r = lax.broadcasted_iota(jnp.int32, (ROW_TILE, ROW_TILE), 0)
    c = lax.broadcasted_iota(jnp.int32, (ROW_TILE, ROW_TILE), 1)
    tri = jnp.logical_and(c <= r, (r // CHUNK) == (c // CHUNK)).astype(F32)
    gc_ref[...] = jnp.dot(tri, g, preferred_element_type=F32, precision=lax.Precision.HIGHEST)


def _gates(ba, alog_row, dtb_row):
    spec = pl.BlockSpec((ROW_TILE, LANES), lambda i: (i, 0))
    vec = pl.BlockSpec((1, LANES), lambda i: (0, 0))
    return pl.pallas_call(
        _gates_body,
        grid=(N_ROW_TILES,),
        in_specs=[spec, vec, vec],
        out_specs=[spec, spec],
        out_shape=[jax.ShapeDtypeStruct((M_ROWS, LANES), F32)] * 2,
        compiler_params=_cparams(("arbitrary",)),
        name="gates",
    )(ba, alog_row, dtb_row)


def _dotb(x, y):
    return jnp.dot(x.astype(BF16), y.astype(BF16), preferred_element_type=F32)


def _inv_masks():
    r = jnp.arange(QUAD)[:, None]
    c = jnp.arange(QUAD)[None, :]
    same = lambda sh: (r >> sh) == (c >> sh)
    masks = [same(3)] + [jnp.logical_and(same(sh), jnp.logical_not(same(sh - 1))) for sh in (4, 5, 6)] + [r == c]
    return jnp.stack(masks).astype(BF16)


def _inv_unit_lower(mats, m_ref):
    ds = [a * m_ref[0] for a in mats]
    d2s = [_dotb(d, d).astype(BF16) for d in ds]
    d4s = [_dotb(d2, d2).astype(BF16) for d2 in d2s]
    ts = [m_ref[4].astype(F32) - d.astype(F32) for d in ds]
    ts = [t + _dotb(t, d2) for t, d2 in zip(ts, d2s)]
    ts = [t + _dotb(t, d4) for t, d4 in zip(ts, d4s)]
    for lvl in (1, 2, 3):
        tbs = [t.astype(BF16) for t in ts]
        ets = [_dotb(a * m_ref[lvl], tb).astype(BF16) for a, tb in zip(mats, tbs)]
        ts = [t - _dotb(tb, et) for t, tb, et in zip(ts, tbs, ets)]
    return ts


def _conv_silu(x_ref, hist_ref, w_ref):
    x = x_ref[...]
    w = w_ref[...]
    xc = jnp.concatenate([hist_ref[...], x], axis=0)
    base = SUBLANES - (CONV_W - 1)
    y = xc[base:base + CHUNK] * w[0:1]
    for j in range(1, CONV_W - 1):
        y = y + xc[base + j:base + j + CHUNK] * w[j:j + 1]
    y = y + x * w[CONV_W - 1:CONV_W]
    hist_ref[...] = x[CHUNK - SUBLANES:CHUNK]
    return _silu(y)


def _gdn_prompt_body(q_ref, k_ref, v_ref, wq_ref, wk_ref, wv_ref, z_ref, gates_ref, nw_ref, m_ref, o_ref, s_ref,
                     hq_ref, hk_ref, hv_ref):
    c = pl.program_id(2)

    @pl.when(c == 0)
    def _():
        s_ref[...] = jnp.zeros_like(s_ref)
        hq_ref[...] = jnp.zeros_like(hq_ref)
        hk_ref[...] = jnp.zeros_like(hk_ref)
        hv_ref[...] = jnp.zeros_like(hv_ref)

    s_old = [s_ref[j] for j in range(GDN_VH)]
    qa = _conv_silu(q_ref, hq_ref, wq_ref)
    ka = _conv_silu(k_ref, hk_ref, wk_ref)
    va = _conv_silu(v_ref, hv_ref, wv_ref)
    z = z_ref[...]
    nw = nw_ref[...]
    head = lambda x, j: x[:, j * HEAD_DIM:(j + 1) * HEAD_DIM]

    pr = 2 * CHUNK
    row = lax.broadcasted_iota(jnp.int32, (pr, pr), 0)
    col = lax.broadcasted_iota(jnp.int32, (pr, pr), 1)
    same_head = (row >> 6) == (col >> 6)
    causal = jnp.logical_and(same_head, row >= col)
    strict = jnp.logical_and(same_head, row > col)
    zero_b = jnp.zeros((pr, pr), BF16)

    blockdiag = lambda x: jnp.concatenate([jnp.concatenate([x[0], zero_b], axis=1),
                                           jnp.concatenate([zero_b, x[1]], axis=1)], axis=0)
    quads = range(GDN_QUADS)
    hrows = lambda x, h: x[h * CHUNK:(h + 1) * CHUNK]

    gcol, bcol, glcol, eg, kst, qst, vst, kts, a_mat, qkd = [], [], [], [], [], [], [], [], [], []
    for qd in quads:
        gates = gates_ref[qd]
        gcols = gates.T
        gcol.append(gcols[:, 0:1])
        bcol.append(gcols[:, 1:2])
        glcol.append(gcols[:, 2:3])
        grow = gates[0:1, :]
        eg.append(jnp.exp(gcol[qd]))
        kps, qps, kpts, a_blk, qkd_blk = [], [], [], [], []
        for pp in range(2):
            q = head(qa, 2 * qd + pp)
            k = head(ka, 2 * qd + pp)
            qn = q * lax.rsqrt(jnp.sum(q * q, axis=-1, keepdims=True) + EPS) * (HEAD_DIM ** -0.5)
            kn = k * lax.rsqrt(jnp.sum(k * k, axis=-1, keepdims=True) + EPS)
            kp = jnp.concatenate([kn, kn], axis=0)
            qp = jnp.concatenate([qn, qn], axis=0)
            kp_t = kp.T.astype(BF16)
            rs = slice(pp * pr, (pp + 1) * pr)
            dec = jnp.exp(jnp.where(causal, gcol[qd][rs] - grow[:, rs], -jnp.inf))
            a_blk.append(jnp.where(strict, bcol[qd][rs] * _dotb(kp, kp_t) * dec, 0.0).astype(BF16))
            qkd_blk.append((_dotb(qp, kp_t) * dec).astype(BF16))
            kps.append(kp)
            qps.append(qp)
            kpts.append(kp_t)
        kst.append(jnp.concatenate(kps, axis=0))
        qst.append(jnp.concatenate(qps, axis=0))
        vst.append(jnp.concatenate([head(va, 4 * qd + h) for h in range(4)], axis=0))
        kts.append(kpts)
        a_mat.append(blockdiag(a_blk))
        qkd.append(blockdiag(qkd_blk))

    t = _inv_unit_lower(a_mat, m_ref)
    sol = [_dotb(t[qd], jnp.concatenate([vst[qd] * bcol[qd], kst[qd] * (bcol[qd] * eg[qd])], axis=1)) for qd in quads]
    qdec = [qst[qd] * eg[qd] for qd in quads]
    res = [[_dotb(jnp.concatenate([hrows(sol[qd][:, HEAD_DIM:], h), hrows(qdec[qd], h)], axis=0), s_old[4 * qd + h])
            for qd in quads] for h in range(4)]
    v_new = [jnp.concatenate([hrows(sol[qd][:, :HEAD_DIM], h) - res[h][qd][:CHUNK] for h in range(4)], axis=0)
             for qd in quads]
    o = [jnp.concatenate([res[h][qd][CHUNK:] for h in range(4)], axis=0) + _dotb(qkd[qd], v_new[qd]) for qd in quads]
    v_sc = [v_new[qd] * jnp.exp(glcol[qd] - gcol[qd]) for qd in quads]
    upd = [[_dotb(kts[qd][pp][:, :CHUNK],
                  jnp.concatenate([hrows(v_sc[qd], 2 * pp), hrows(v_sc[qd], 2 * pp + 1)], axis=1))
            for qd in quads] for pp in range(2)]
    for qd in quads:
        for h in range(4):
            j = 4 * qd + h
            oh = hrows(o[qd], h)
            on = oh * lax.rsqrt(jnp.mean(oh * oh, axis=-1, keepdims=True) + EPS) * nw * _silu(head(z, j))
            o_ref[:, j * HEAD_DIM:(j + 1) * HEAD_DIM] = on.astype(o_ref.dtype)
    for qd in quads:
        for h in range(4):
            gl = glcol[qd][h * CHUNK:h * CHUNK + 1, :]
            s_ref[4 * qd + h] = s_old[4 * qd + h] * jnp.exp(gl) + head(upd[h // 2][qd], h % 2)


def _gdn_gate_rows(gb, gcum):
    nq = V_HEADS // 4
    to_quads = lambda x: x.reshape(BATCH, N_CHUNKS, CHUNK, nq, 4).transpose(0, 1, 3, 4, 2)
    g = to_quads(gcum[:N_PROMPT, V_HEADS:2 * V_HEADS])
    beta = to_quads(gb[:N_PROMPT, :V_HEADS])
    glast = jnp.broadcast_to(g[..., CHUNK - 1:], g.shape)
    rows = jnp.stack([g, beta, glast], axis=3).reshape(BATCH, N_CHUNKS, nq, 3, QUAD)
    return jnp.pad(rows, ((0, 0), (0, 0), (0, 0), (0, SUBLANES - 3), (0, 0)))


def _gdn_prompt(qkv, z, conv_w, gates, norm_w, *, layer):
    qw = GDN_PAIRS * HEAD_DIM
    vw = GDN_VH * HEAD_DIM
    kb0 = QK_W // qw
    vb0 = 2 * QK_W // vw
    rowblk = lambda b, h, c: b * N_CHUNKS + c
    return pl.pallas_call(
        _gdn_prompt_body,
        grid=(BATCH, GDN_GROUPS, N_CHUNKS),
        in_specs=[
            pl.BlockSpec((CHUNK, qw), lambda b, h, c: (rowblk(b, h, c), h)),
            pl.BlockSpec((CHUNK, qw), lambda b, h, c: (rowblk(b, h, c), kb0 + h)),
            pl.BlockSpec((CHUNK, vw), lambda b, h, c: (rowblk(b, h, c), vb0 + h)),
            pl.BlockSpec((None, CONV_W, qw), lambda b, h, c: (layer, 0, h)),
            pl.BlockSpec((None, CONV_W, qw), lambda b, h, c: (layer, 0, kb0 + h)),
            pl.BlockSpec((None, CONV_W, vw), lambda b, h, c: (layer, 0, vb0 + h)),
            pl.BlockSpec((CHUNK, vw), lambda b, h, c: (rowblk(b, h, c), h)),
            pl.BlockSpec((None, None, GDN_QUADS, SUBLANES, QUAD), lambda b, h, c: (b, c, h, 0, 0)),
            pl.BlockSpec((None, 1, HEAD_DIM), lambda b, h, c: (layer, 0, 0)),
            pl.BlockSpec((5, QUAD, QUAD), lambda b, h, c: (0, 0, 0)),
        ],
        out_specs=[
            pl.BlockSpec((CHUNK, vw), lambda b, h, c: (rowblk(b, h, c), h)),
            pl.BlockSpec((None, GDN_VH, HEAD_DIM, HEAD_DIM), lambda b, h, c: (b, h, 0, 0)),
        ],
        out_shape=[jax.ShapeDtypeStruct((N_PROMPT, V_W), BF16),
                   jax.ShapeDtypeStruct((BATCH, V_HEADS, HEAD_DIM, HEAD_DIM), F32)],
        scratch_shapes=[pltpu.VMEM((SUBLANES, qw), F32), pltpu.VMEM((SUBLANES, qw), F32),
                        pltpu.VMEM((SUBLANES, vw), F32)],
        compiler_params=_cparams(("arbitrary", "arbitrary", "arbitrary")),
        name="gdn_prompt",
    )(qkv, qkv, qkv, conv_w, conv_w, conv_w, z, gates, norm_w.reshape(DEPTH, 1, HEAD_DIM), _inv_masks())


def _gdn_sample_prep_body(x_ref, c_ref, w_ref, o_ref, *, normalize):
    w = w_ref[...]
    y = c_ref[0] * w[0:1]
    for j in range(1, CONV_W - 1):
        y = y + c_ref[j] * w[j:j + 1]
    y = y + x_ref[...] * w[CONV_W - 1:CONV_W]
    y = _silu(y)
    if normalize:
        is_q = pl.program_id(0) < QK_W // y.shape[1]
        scale = jnp.where(is_q, HEAD_DIM ** -0.5, 1.0)
        for h in range(y.shape[1] // HEAD_DIM):
            yh = y[:, h * HEAD_DIM:(h + 1) * HEAD_DIM]
            yh = yh * lax.rsqrt(jnp.sum(yh * yh, axis=-1, keepdims=True) + EPS) * scale
            o_ref[:, h * HEAD_DIM:(h + 1) * HEAD_DIM] = yh
    else:
        o_ref[...] = y


def _gdn_sample_prep(qkv, cache_t, conv_w, *, layer, col0, n, normalize):
    tn = 1024
    cb0 = col0 // tn
    rb = N_PROMPT // DEC_BATCH
    return pl.pallas_call(
        functools.partial(_gdn_sample_prep_body, normalize=normalize),
        grid=(n // tn,),
        in_specs=[
            pl.BlockSpec((DEC_BATCH, tn), lambda j: (rb, cb0 + j)),
            pl.BlockSpec((None, CONV_W - 1, DEC_BATCH, tn), lambda j: (layer, 0, 0, cb0 + j)),
            pl.BlockSpec((None, CONV_W, tn), lambda j: (layer, 0, cb0 + j)),
        ],
        out_specs=pl.BlockSpec((DEC_BATCH, tn), lambda j: (0, j)),
        out_shape=jax.ShapeDtypeStruct((DEC_BATCH, n), F32),
        compiler_params=_cparams(("arbitrary",)),
        name="gdn_sample_prep",
    )(qkv, cache_t, conv_w)


def _gdn_sample_body(s_ref, qt_ref, kt_ref, v_ref, z_ref, eg_ref, beta_ref, nw_ref, so_ref, o_ref):
    qt = qt_ref[...]
    kt = kt_ref[...]
    v = v_ref[...]
    z = z_ref[...]
    egs = eg_ref[...]
    betas = beta_ref[...]
    nw = nw_ref[...]
    for h in range(V_HEADS):
        hq = h // 2
        kc = kt[:, hq:hq + 1]
        qc = qt[:, hq:hq + 1]
        s = s_ref[h]
        eg = egs[:, h:h + 1]
        beta = betas[:, h:h + 1]
        ks = jnp.sum(s * kc, axis=0, keepdims=True)
        qs = jnp.sum(s * qc, axis=0, keepdims=True)
        vh = v[:, h * HEAD_DIM:(h + 1) * HEAD_DIM]
        v_new = beta * vh - (beta * eg) * ks
        qk = jnp.sum(qc * kc, axis=0, keepdims=True)
        o = eg * qs + qk * v_new
        so_ref[h] = s * eg + kc * v_new
        zz = z[:, h * HEAD_DIM:(h + 1) * HEAD_DIM]
        on = o * lax.rsqrt(jnp.mean(o * o, axis=-1, keepdims=True) + EPS) * nw * _silu(zz)
        o_ref[:, h * HEAD_DIM:(h + 1) * HEAD_DIM] = on.astype(o_ref.dtype)


def _gdn_sample_inplace_body(*refs):
    _gdn_sample_body(*refs[:8], *refs[9:])


def _gdn_sample(state, acc, qt, kt, v, z, eg, beta, norm_w, *, layer):
    r3 = lambda w: pl.BlockSpec((None, 1, w), lambda b: (b, 0, 0))
    return pl.pallas_call(
        _gdn_sample_inplace_body,
        grid=(DEC_BATCH,),
        in_specs=[
            pl.BlockSpec((None, None, V_HEADS, HEAD_DIM, HEAD_DIM), lambda b: (layer, b, 0, 0, 0)),
            pl.BlockSpec((None, HEAD_DIM, QK_HEADS), lambda b: (b, 0, 0)),
            pl.BlockSpec((None, HEAD_DIM, QK_HEADS), lambda b: (b, 0, 0)),
            r3(V_W), r3(V_W), r3(V_HEADS), r3(V_HEADS),
            pl.BlockSpec((None, 1, HEAD_DIM), lambda b: (layer, 0, 0)),
            pl.BlockSpec(memory_space=pl.ANY),
        ],
        out_specs=[
            pl.BlockSpec((None, None, V_HEADS, HEAD_DIM, HEAD_DIM), lambda b: (layer, b, 0, 0, 0)),
            r3(V_W),
        ],
        out_shape=[jax.ShapeDtypeStruct((DEPTH, DEC_BATCH, V_HEADS, HEAD_DIM, HEAD_DIM), F32),
                   jax.ShapeDtypeStruct((DEC_BATCH, 1, V_W), BF16)],
        input_output_aliases={8: 0},
        compiler_params=_cparams(("arbitrary",)),
        name="gdn_sample",
    )(state, qt, kt, v, z, eg, beta, norm_w.reshape(DEPTH, 1, HEAD_DIM), acc)


def _s5_in_proj(ub, wb_ref, re_ref, im_ref):
    for j in range(S5_NBLK):
        r = jnp.dot(ub[:, j * LANES:(j + 1) * LANES], wb_ref[j], preferred_element_type=F32)
        re_ref[:, j * S5_BS:(j + 1) * S5_BS] = r[:, :S5_BS]
        im_ref[:, j * S5_BS:(j + 1) * S5_BS] = r[:, S5_BS:]


def _s5_out_proj(u, re_ref, im_ref, wc_ref, d_ref, y_ref):
    for j in range(S5_NBLK):
        xr = re_ref[:, j * S5_BS:(j + 1) * S5_BS].astype(BF16)
        xi = im_ref[:, j * S5_BS:(j + 1) * S5_BS].astype(BF16)
        y = jnp.dot(xr, wc_ref[j, :S5_BS], preferred_element_type=F32)
        y = y + jnp.dot(xi, wc_ref[j, S5_BS:], preferred_element_type=F32)
        y = y + d_ref[:, j * LANES:(j + 1) * LANES] * u[:, j * LANES:(j + 1) * LANES]
        y_ref[:, j * LANES:(j + 1) * LANES] = _gelu_tanh(y)


def _s5_prompt_body(*refs):
    u_refs = refs[:BATCH]
    wb_ref, wc_ref, lr_ref, li_ref, d_ref, y_ref, sre_ref, sim_ref, re_ref, im_ref, utm_ref, ytm_ref = refs[BATCH:]

    @pl.when(pl.program_id(0) == 0)
    def _():
        sre_ref[...] = jnp.zeros_like(sre_ref)
        sim_ref[...] = jnp.zeros_like(sim_ref)
        utm_ref[...] = jnp.zeros_like(utm_ref)

    for b in range(BATCH):
        for j in range(S5_NBLK):
            utm_ref[j, pl.ds(b, S5_TT, stride=SUBLANES), :] = u_refs[b][:, j * LANES:(j + 1) * LANES]
    for j in range(S5_NBLK):
        r = jnp.dot(utm_ref[j].astype(BF16), wb_ref[j], preferred_element_type=F32)
        re_ref[:, j * S5_BS:(j + 1) * S5_BS] = r[:, :S5_BS]
        im_ref[:, j * S5_BS:(j + 1) * S5_BS] = r[:, S5_BS:]
    cw = 1024
    for cb in range(S5_STATES // cw):
        cols = slice(cb * cw, (cb + 1) * cw)
        lr = lr_ref[:, cols]
        li = li_ref[:, cols]

        def step(t, carry):
            xr, xi = carry
            rows = pl.ds(pl.multiple_of(t * SUBLANES, SUBLANES), SUBLANES)
            nr = lr * xr - li * xi + re_ref[rows, cols]
            ni = lr * xi + li * xr + im_ref[rows, cols]
            re_ref[rows, cols] = nr
            im_ref[rows, cols] = ni
            return nr, ni

        xr, xi = lax.fori_loop(0, S5_TT, step, (sre_ref[:, cols], sim_ref[:, cols]), unroll=2)
        sre_ref[:, cols] = xr
        sim_ref[:, cols] = xi
    for j in range(S5_NBLK):
        xr = re_ref[:, j * S5_BS:(j + 1) * S5_BS].astype(BF16)
        xi = im_ref[:, j * S5_BS:(j + 1) * S5_BS].astype(BF16)
        y = jnp.dot(xr, wc_ref[j, :S5_BS], preferred_element_type=F32)
        y = y + jnp.dot(xi, wc_ref[j, S5_BS:], preferred_element_type=F32)
        y = y + d_ref[:, j * LANES:(j + 1) * LANES] * utm_ref[j]
        ytm_ref[j] = _gelu_tanh(y)
    for b in range(BATCH):
        for j in range(S5_NBLK):
            y_ref[b, :, j * LANES:(j + 1) * LANES] = ytm_ref[j, pl.ds(b, S5_TT, stride=SUBLANES), :]


def _s5_prompt(tail, wb, wc, lam_re8, lam_im8, d_row, *, layer):
    rows = S5_TT * SUBLANES
    steps = SEQ // S5_TT
    full = lambda shape: pl.BlockSpec(shape, lambda t: (0,) * len(shape))
    lay = lambda shape: pl.BlockSpec((None,) + shape, lambda t: (layer,) + (0,) * len(shape))
    return pl.pallas_call(
        _s5_prompt_body,
        grid=(steps,),
        in_specs=[pl.BlockSpec((S5_TT, D_MODEL), functools.partial(lambda b, t: (b * steps + t, 0), b))
                  for b in range(BATCH)] + [
            lay((S5_NBLK, LANES, 2 * S5_BS)),
            lay((S5_NBLK, 2 * S5_BS, LANES)),
            lay((SUBLANES, S5_STATES)),
            lay((SUBLANES, S5_STATES)),
            lay((1, D_MODEL)),
        ],
        out_specs=[
            pl.BlockSpec((BATCH, S5_TT, D_MODEL), lambda t: (0, t, 0)),
            full((SUBLANES, S5_STATES)),
            full((SUBLANES, S5_STATES)),
        ],
        out_shape=[jax.ShapeDtypeStruct((BATCH, SEQ, D_MODEL), F32),
                   jax.ShapeDtypeStruct((SUBLANES, S5_STATES), F32),
                   jax.ShapeDtypeStruct((SUBLANES, S5_STATES), F32)],
        scratch_shapes=[pltpu.VMEM((rows, S5_STATES), F32), pltpu.VMEM((rows, S5_STATES), F32),
                        pltpu.VMEM((S5_NBLK, rows, LANES), F32), pltpu.VMEM((S5_NBLK, rows, LANES), F32)],
        compiler_params=_cparams(("arbitrary",)),
        name="s5_prompt",
    )(*([tail] * BATCH), wb, wc, lam_re8, lam_im8, d_row)


def _s5_sample_body(u_ref, x0r_ref, x0i_ref, wb_ref, wc_ref, lr_ref, li_ref, d_ref, y_ref, x1r_ref, x1i_ref):
    u = u_ref[...]
    _s5_in_proj(u.astype(BF16), wb_ref, x1r_ref, x1i_ref)
    lr = lr_ref[...]
    li = li_ref[...]
    xr = x0r_ref[...]
    xi = x0i_ref[...]
    x1r_ref[...] = x1r_ref[...] + (lr * xr - li * xi)
    x1i_ref[...] = x1i_ref[...] + (lr * xi + li * xr)
    _s5_out_proj(u, x1r_ref, x1i_ref, wc_ref, d_ref, y_ref)


def _s5_sample(u_all, x0r, x0i, wb, wc, lam_re, lam_im, d_row, *, layer):
    full = lambda shape: pl.BlockSpec(shape, lambda i: (0,) * len(shape))
    lay = lambda shape: pl.BlockSpec((None,) + shape, lambda i: (layer,) + (0,) * len(shape))
    st = lay((DEC_BATCH, S5_STATES))
    return pl.pallas_call(
        _s5_sample_body,
        grid=(1,),
        in_specs=[
            pl.BlockSpec((DEC_BATCH, D_MODEL), lambda i: (N_PROMPT // DEC_BATCH, 0)),
            st, st,
            lay((S5_NBLK, LANES, 2 * S5_BS)),
            lay((S5_NBLK, 2 * S5_BS, LANES)),
            lay((1, S5_STATES)), lay((1, S5_STATES)), lay((1, D_MODEL)),
        ],
        out_specs=[full((DEC_BATCH, D_MODEL)), full((DEC_BATCH, S5_STATES)), full((DEC_BATCH, S5_STATES))],
        out_shape=[jax.ShapeDtypeStruct((DEC_BATCH, D_MODEL), F32),
                   jax.ShapeDtypeStruct((DEC_BATCH, S5_STATES), F32),
                   jax.ShapeDtypeStruct((DEC_BATCH, S5_STATES), F32)],
        compiler_params=_cparams(("arbitrary",)),
        name="s5_sample",
    )(u_all, x0r, x0i, wb, wc, lam_re, lam_im, d_row)


def _s5_params(lam_re, lam_im, log_dt, b_re, b_im, c_re, c_im, d_skip):
    lam = lax.complex(lam_re, lam_im)
    dt = jnp.exp(log_dt)[:, None]
    lam_bar = jnp.exp(lam * dt)
    b_bar = ((lam_bar - 1.0) / lam)[..., None] * lax.complex(b_re, b_im)
    eye = jnp.eye(S5_BLK, dtype=F32)

    def in_blocks(b):
        b = b.reshape(S5_NBLK, S5_BLK, S5_N, S5_P)
        return jnp.einsum("jgnp,gh->jgphn", b, eye).reshape(S5_NBLK, S5_BLK * S5_P, S5_BS)

    def out_blocks(c):
        c = c.reshape(S5_NBLK, S5_BLK, S5_P, S5_N)
        return jnp.einsum("jgpn,gh->jgnhp", c, eye).reshape(S5_NBLK, S5_BS, S5_BLK * S5_P)

    wb = jnp.concatenate([in_blocks(b_bar.real), in_blocks(b_bar.imag)], axis=2).astype(BF16)
    wc = jnp.concatenate([out_blocks(c_re), -out_blocks(c_im)], axis=1).astype(BF16)
    return (wb, wc, lam_bar.real.reshape(1, S5_STATES), lam_bar.imag.reshape(1, S5_STATES),
            d_skip.reshape(1, D_MODEL))


def _router_body(h_ref, w_ref, b_ref, idx_ref, wt_ref):
    logits = jnp.dot(h_ref[...].astype(BF16), w_ref[...].astype(BF16), preferred_element_type=F32)
    scores = _sigmoid(logits)
    biased = scores + b_ref[...]
    lane = lax.broadcasted_iota(jnp.int32, biased.shape, 1)
    kcol = lax.broadcasted_iota(jnp.int32, (biased.shape[0], TOP_K), 1)
    idx = jnp.zeros((biased.shape[0], TOP_K), jnp.int32)
    sel = jnp.zeros((biased.shape[0], TOP_K), F32)
    for k in range(TOP_K):
        m = jnp.max(biased, axis=-1, keepdims=True)
        first = jnp.min(jnp.where(biased == m, lane, N_EXPERTS), axis=-1, keepdims=True)
        hit = lane == first
        s = jnp.sum(jnp.where(hit, scores, 0.0), axis=-1, keepdims=True)
        idx = jnp.where(kcol == k, first, idx)
        sel = jnp.where(kcol == k, s, sel)
        biased = jnp.where(hit, -jnp.inf, biased)
    idx_ref[...] = idx
    wt_ref[...] = sel / jnp.sum(sel, axis=-1, keepdims=True) * ROUTED_SCALE


def _router(h, w_router, router_bias, *, layer):
    tm = MM_TM
    return pl.pallas_call(
        _router_body,
        grid=(M_ROWS // tm,),
        in_specs=[
            pl.BlockSpec((tm, D_MODEL), lambda i: (i, 0)),
            pl.BlockSpec((None, D_MODEL, N_EXPERTS), lambda i: (layer, 0, 0)),
            pl.BlockSpec((None, 1, N_EXPERTS), lambda i: (layer, 0, 0)),
        ],
        out_specs=[pl.BlockSpec((tm, TOP_K), lambda i: (i, 0))] * 2,
        out_shape=[jax.ShapeDtypeStruct((M_ROWS, TOP_K), jnp.int32),
                   jax.ShapeDtypeStruct((M_ROWS, TOP_K), F32)],
        compiler_params=_cparams(("arbitrary",)),
        name="router",
    )(h, w_router, router_bias.reshape(DEPTH, 1, N_EXPERTS))


def _routing_tables(top_idx):
    flat_e = top_idx.reshape(-1)
    order = jnp.argsort(flat_e).astype(jnp.int32)
    experts = jnp.arange(N_EXPERTS, dtype=jnp.int32)
    sizes = jnp.sum((flat_e[:, None] == experts[None, :]).astype(jnp.int32), axis=0)
    starts = jnp.cumsum(sizes) - sizes
    padded = (sizes + MOE_BS - 1) // MOE_BS * MOE_BS
    pad_ends = jnp.cumsum(padded)
    pad_starts = pad_ends - padded
    i = jnp.arange(N_ASSIGN, dtype=jnp.int32)[:, None]
    mine = jnp.logical_and(i >= starts[None, :], i < (starts + sizes)[None, :])
    dest = i[:, 0] + jnp.sum(jnp.where(mine, (pad_starts - starts)[None, :], 0), axis=1)
    _, pos = lax.sort((order, dest.astype(jnp.int32)), num_keys=1)
    blk_start = jnp.arange(MOE_BLOCKS, dtype=jnp.int32) * MOE_BS
    block_e = jnp.minimum(jnp.sum(blk_start[:, None] >= pad_ends[None, :], axis=1), N_EXPERTS - 1).astype(jnp.int32)
    sorted_tok = jnp.concatenate([order // TOP_K, jnp.zeros((MOE_BS,), jnp.int32)])
    off = jnp.clip(starts[block_e] + blk_start - pad_starts[block_e], 0, N_ASSIGN)
    return sorted_tok, off.astype(jnp.int32), pos, block_e


def _to_slabs(x):
    return x.reshape(x.shape[0] * ROW_SLAB, LANES)


def _load_slab_rows(ref, n):
    return jnp.concatenate([ref[pl.ds(j, n, stride=SLAB_PITCH), :] for j in range(ROW_SLAB)], axis=1)


def _expert_gather(stok_ref, base, h_hbm, xbuf, sem, slot):
    for r in range(MOE_BS):
        src = h_hbm.at[pl.ds(pl.multiple_of(stok_ref[base + r] * ROW_SLAB, ROW_SLAB), ROW_SLAB), :]
        pltpu.make_async_copy(src, xbuf.at[slot, pl.ds(r * SLAB_PITCH, ROW_SLAB), :], sem.at[slot]).start()


def _expert_gather_wait(h_hbm, xbuf, sem, slot):
    pltpu.make_async_copy(h_hbm.at[pl.ds(0, MOE_BS * ROW_SLAB), :], xbuf.at[slot, pl.ds(0, MOE_BS * ROW_SLAB), :],
                          sem.at[slot]).wait()


def _expert_weight_copies(layer, e, wslot, w_hbm, wbuf, wsem):
    return [pltpu.make_async_copy(w.at[layer, e], buf.at[wslot], wsem.at[wslot]) for w, buf in zip(w_hbm, wbuf)]


def _experts_body(be_ref, first_ref, nxt_ref, wslot_ref, off_ref, stok_ref, h_hbm, wg_hbm, wu_hbm, wd_hbm, o_ref,
                  xbuf, sem, wgf, wuf, wdf, wsem, wgb, wub, wdb, *, layer):
    b = pl.program_id(0)
    nb = pl.num_programs(0)
    slot = lax.rem(b, 2)
    w_hbm = (wg_hbm, wu_hbm, wd_hbm)
    wbuf = (wgf, wuf, wdf)
    ws = wslot_ref[b]

    @pl.when(b == 0)
    def _():
        _expert_gather(stok_ref, off_ref[0], h_hbm, xbuf, sem, 0)
        for c in _expert_weight_copies(layer, be_ref[0], 0, w_hbm, wbuf, wsem):
            c.start(priority=1)

    @pl.when(first_ref[b] == 1)
    def _():
        for c in _expert_weight_copies(layer, be_ref[b], ws, w_hbm, wbuf, wsem):
            c.wait()

        @pl.when(nxt_ref[b] >= 0)
        def _():
            for c in _expert_weight_copies(layer, nxt_ref[b], 1 - ws, w_hbm, wbuf, wsem):
                c.start(priority=1)

        wgb[...] = wgf[ws].astype(BF16)
        wub[...] = wuf[ws].astype(BF16)
        wdb[...] = wdf[ws].astype(BF16)

    _expert_gather_wait(h_hbm, xbuf, sem, slot)
    _expert_gather(stok_ref, off_ref[jnp.minimum(b + 1, nb - 1)], h_hbm, xbuf, sem, 1 - slot)
    x = _load_slab_rows(xbuf.at[slot], MOE_BS).astype(BF16)
    g = jnp.dot(x, wgb[...], preferred_element_type=F32)
    u = jnp.dot(x, wub[...], preferred_element_type=F32)
    a = (_silu(g) * u).astype(BF16)
    y = jnp.dot(a, wdb[...], preferred_element_type=F32)
    for j in range(ROW_SLAB):
        o_ref[pl.ds(j, MOE_BS, stride=ROW_SLAB), :] = y[:, j * LANES:(j + 1) * LANES]

    @pl.when(b == nb - 1)
    def _():
        _expert_gather_wait(h_hbm, xbuf, sem, 1 - slot)


def _experts(h_slabs, sorted_tok, off, block_e, w_gate, w_up, w_down, *, layer):
    prev_e = jnp.concatenate([jnp.full((1,), -1, jnp.int32), block_e[:-1]])
    first = (block_e != prev_e).astype(jnp.int32)
    run = jnp.cumsum(first) - 1
    blk = jnp.arange(MOE_BLOCKS, dtype=jnp.int32)
    later_first = jnp.logical_and(first[None, :] == 1, blk[None, :] > blk[:, None])
    nxt_blk = jnp.min(jnp.where(later_first, blk[None, :], MOE_BLOCKS), axis=1)
    nxt = jnp.where(nxt_blk < MOE_BLOCKS, block_e[jnp.minimum(nxt_blk, MOE_BLOCKS - 1)], -1).astype(jnp.int32)
    wslot = (run % 2).astype(jnp.int32)

    any_spec = pl.BlockSpec(memory_space=pl.ANY)
    grid_spec = pltpu.PrefetchScalarGridSpec(
        num_scalar_prefetch=6,
        grid=(MOE_BLOCKS,),
        in_specs=[any_spec, any_spec, any_spec, any_spec],
        out_specs=pl.BlockSpec((MOE_BS * ROW_SLAB, LANES), lambda b, *_: (b, 0)),
        scratch_shapes=[
            pltpu.VMEM((2, MOE_BS * SLAB_PITCH, LANES), F32),
            pltpu.SemaphoreType.DMA((2,)),
            pltpu.VMEM((2, D_MODEL, D_EXPERT), F32),
            pltpu.VMEM((2, D_MODEL, D_EXPERT), F32),
            pltpu.VMEM((2, D_EXPERT, D_MODEL), F32),
            pltpu.SemaphoreType.DMA((2,)),
            pltpu.VMEM((D_MODEL, D_EXPERT), BF16),
            pltpu.VMEM((D_MODEL, D_EXPERT), BF16),
            pltpu.VMEM((D_EXPERT, D_MODEL), BF16),
        ],
    )
    return pl.pallas_call(
        functools.partial(_experts_body, layer=layer),
        grid_spec=grid_spec,
        out_shape=jax.ShapeDtypeStruct((MOE_ROWS * ROW_SLAB, LANES), F32),
        compiler_params=_cparams(("arbitrary",)),
        name="experts",
    )(block_e, first, nxt, wslot, off, sorted_tok, h_slabs, w_gate, w_up, w_down)


def _shared_body(h_ref, wg_ref, wu_ref, wd_ref, o_ref, wgb, wub, wdb):
    @pl.when(pl.program_id(0) == 0)
    def _():
        wgb[...] = wg_ref[...].astype(BF16)
        wub[...] = wu_ref[...].astype(BF16)
        wdb[...] = wd_ref[...].astype(BF16)

    x = h_ref[...].astype(BF16)
    g = jnp.dot(x, wgb[...], preferred_element_type=F32)
    u = jnp.dot(x, wub[...], preferred_element_type=F32)
    a = (_silu(g) * u).astype(BF16)
    o_ref[...] = jnp.dot(a, wdb[...], preferred_element_type=F32)


def _shared_expert(h, w_gate, w_up, w_down, *, layer):
    tm = MM_TM
    return pl.pallas_call(
        _shared_body,
        grid=(M_ROWS // tm,),
        in_specs=[
            pl.BlockSpec((tm, D_MODEL), lambda i: (i, 0)),
            pl.BlockSpec((None, D_MODEL, D_EXPERT), lambda i: (layer, 0, 0)),
            pl.BlockSpec((None, D_MODEL, D_EXPERT), lambda i: (layer, 0, 0)),
            pl.BlockSpec((None, D_EXPERT, D_MODEL), lambda i: (layer, 0, 0)),
        ],
        out_specs=pl.BlockSpec((tm, D_MODEL), lambda i: (i, 0)),
        out_shape=jax.ShapeDtypeStruct((M_ROWS, D_MODEL), F32),
        scratch_shapes=[pltpu.VMEM((D_MODEL, D_EXPERT), BF16), pltpu.VMEM((D_MODEL, D_EXPERT), BF16),
                        pltpu.VMEM((D_EXPERT, D_MODEL), BF16)],
        compiler_params=_cparams(("arbitrary",)),
        name="shared_expert",
    )(h, w_gate, w_up, w_down)


def _combine_gather(pos_ref, y_hbm, buf, sem, slot):
    def issue(t, carry):
        for k in range(TOP_K):
            p = pos_ref[0, t * TOP_K + k]
            src = y_hbm.at[pl.ds(pl.multiple_of(p * ROW_SLAB, ROW_SLAB), ROW_SLAB), :]
            dst = buf.at[slot, k, pl.ds(t * SLAB_PITCH, ROW_SLAB), :]
            pltpu.make_async_copy(src, dst, sem.at[slot]).start(priority=k % 2)
        return carry

    lax.fori_loop(0, COMB_T, issue, 0, unroll=2)


def _combine_gather_wait(y_hbm, buf, sem, slot):
    for k in range(TOP_K):
        pltpu.make_async_copy(y_hbm.at[pl.ds(0, COMB_T * ROW_SLAB), :],
                              buf.at[slot, k, pl.ds(0, COMB_T * ROW_SLAB), :], sem.at[slot]).wait()


def _combine_body(pos_ref, posn_ref, y_hbm, wt_ref, sh_ref, o_ref, buf, sem):
    i = pl.program_id(0)
    slot = lax.rem(i, 2)

    @pl.when(i == 0)
    def _():
        _combine_gather(pos_ref, y_hbm, buf, sem, 0)

    @pl.when(i + 1 < pl.num_programs(0))
    def _():
        _combine_gather(posn_ref, y_hbm, buf, sem, 1 - slot)

    _combine_gather_wait(y_hbm, buf, sem, slot)
    wt = wt_ref[...]
    wk = [jnp.broadcast_to(wt[:, k:k + 1], (COMB_T, LANES)) for k in range(TOP_K)]
    for j in range(ROW_SLAB):
        cols = slice(j * LANES, (j + 1) * LANES)
        acc = sh_ref[:, cols]
        for k in range(TOP_K):
            acc = acc + buf[slot, k, pl.ds(j, COMB_T, stride=SLAB_PITCH), :] * wk[k]
        o_ref[:, cols] = acc


def _combine(y_slabs, pos, top_w, shared):
    nt = M_ROWS // COMB_T
    pos3 = pos.reshape(nt, 1, COMB_T * TOP_K)
    smem_pos = lambda f: pl.BlockSpec((None, 1, COMB_T * TOP_K), f, memory_space=pltpu.SMEM)
    return pl.pallas_call(
        _combine_body,
        grid=(nt,),
        in_specs=[
            smem_pos(lambda i: (i, 0, 0)),
            smem_pos(lambda i: (jnp.minimum(i + 1, nt - 1), 0, 0)),
            pl.BlockSpec(memory_space=pl.ANY),
            pl.BlockSpec((COMB_T, TOP_K), lambda i: (i, 0)),
            pl.BlockSpec((COMB_T, D_MODEL), lambda i: (i, 0)),
        ],
        out_specs=pl.BlockSpec((COMB_T, D_MODEL), lambda i: (i, 0)),
        out_shape=jax.ShapeDtypeStruct((M_ROWS, D_MODEL), F32),
        scratch_shapes=[pltpu.VMEM((2, TOP_K, COMB_T * SLAB_PITCH, LANES), F32), pltpu.SemaphoreType.DMA((2,))],
        compiler_params=_cparams(("arbitrary",)),
        name="combine",
    )(pos3, pos3, y_slabs, top_w, shared)


def _mixer(l, h1, p, cache_t, state_gdn, gdn_acc, s5_x0r, s5_x0i):
    w_in = p["w_in"]
    qkv = _mm(h1, w_in, layer=l, col0=0, n=QKV_W, tn=1024, name="in_qkv")
    z = _mm(h1, w_in, layer=l, col0=COL_Z, n=V_W, tn=1024, name="in_z")
    ba = _mm(h1, p["w_ba"], layer=l, col0=0, n=LANES, tn=LANES, name="in_ba")
    tail = _mm(h1, p["w_tail"], layer=l, col0=0, n=3 * D_MODEL, tn=1024, name="in_tail")

    gb, gcum = _gates(ba, p["alog_row"][l], p["dtb_row"][l])
    o_p, gdn_p = _gdn_prompt(qkv, z, p["conv_w"], _gdn_gate_rows(gb, gcum), p["gdn_norm_w"], layer=l)

    qk_s = _gdn_sample_prep(qkv, cache_t, p["conv_w"], layer=l, col0=0, n=2 * QK_W, normalize=True)
    v_s = _gdn_sample_prep(qkv, cache_t, p["conv_w"], layer=l, col0=2 * QK_W, n=V_W, normalize=False)
    col_form = lambda x: x.reshape(DEC_BATCH, QK_HEADS, HEAD_DIM).transpose(0, 2, 1)
    qt = col_form(qk_s[:, :QK_W])
    kt = col_form(qk_s[:, QK_W:])
    beta_s = gb[N_PROMPT:, :V_HEADS].reshape(DEC_BATCH, 1, V_HEADS)
    eg_s = jnp.exp(gb[N_PROMPT:, V_HEADS:2 * V_HEADS]).reshape(DEC_BATCH, 1, V_HEADS)
    z_s = z[N_PROMPT:].reshape(DEC_BATCH, 1, V_W)
    gdn_s, o_s = _gdn_sample(state_gdn, gdn_acc, qt, kt, v_s.reshape(DEC_BATCH, 1, V_W), z_s, eg_s, beta_s,
                             p["gdn_norm_w"], layer=l)
    o_all = jnp.concatenate([o_p, o_s.reshape(DEC_BATCH, V_W)], axis=0)
    y_a = _mm(o_all, p["w_br_a"], layer=l, col0=0, n=D_MODEL, tn=512, name="br_a")

    wb, wc, lam_re, lam_im, d_row = p["s5"]
    lam_re8 = jnp.broadcast_to(lam_re, (DEPTH, SUBLANES, S5_STATES))
    lam_im8 = jnp.broadcast_to(lam_im, (DEPTH, SUBLANES, S5_STATES))
    y_p, re_p, im_p = _s5_prompt(tail, wb, wc, lam_re8, lam_im8, d_row, layer=l)
    y_s, re_s, im_s = _s5_sample(tail, s5_x0r, s5_x0i, wb, wc, lam_re, lam_im, d_row, layer=l)
    y_all = jnp.concatenate([y_p.reshape(N_PROMPT, D_MODEL), y_s], axis=0)
    ys_spec = pl.BlockSpec((MM_TM, 1024), lambda j, i: (i, j))
    y_glu = _mm(y_all, p["w_glu"], layer=l, col0=0, n=D_MODEL, tn=1024, out_dtype=BF16, epilogue="glu",
                extra=(y_all,), extra_specs=(ys_spec,), name="glu")

    tn = 1024
    nb = D_MODEL // tn
    merged = _mm(y_glu, p["w_br_b"], layer=l, col0=0, n=D_MODEL, tn=tn, out_dtype=BF16, epilogue="merge",
                 extra=(y_a, tail, tail),
                 extra_specs=(pl.BlockSpec((MM_TM, tn), lambda j, i: (i, j)),
                              pl.BlockSpec((MM_TM, tn), lambda j, i: (i, nb + j)),
                              pl.BlockSpec((MM_TM, tn), lambda j, i: (i, 2 * nb + j))),
                 name="br_b_merge")
    mix = _mm(merged, p["w_out"], layer=l, col0=0, n=D_MODEL, tn=1024, name="out_proj")

    conv_p = jnp.stack([qkv[(b + 1) * SEQ - (CONV_W - 1):(b + 1) * SEQ] for b in range(BATCH)])
    conv_s = jnp.concatenate([cache_t[l, 1:].transpose(1, 0, 2), qkv[N_PROMPT:, None, :]], axis=1)
    states = (conv_p, gdn_p,
              re_p[:BATCH].reshape(BATCH, S5_G, S5_N), im_p[:BATCH].reshape(BATCH, S5_G, S5_N),
              conv_s, gdn_s,
              re_s.reshape(DEC_BATCH, S5_G, S5_N), im_s.reshape(DEC_BATCH, S5_G, S5_N))
    return mix, states


def _moe(l, h2, p):
    top_idx, top_w = _router(h2, p["w_router"], p["router_bias"], layer=l)
    sorted_tok, off, pos, block_e = _routing_tables(top_idx)
    y_slabs = _experts(_to_slabs(h2), sorted_tok, off, block_e, p["w_exp_gate"], p["w_exp_up"], p["w_exp_down"],
                       layer=l)
    shared = _shared_expert(h2, p["w_sh_gate"], p["w_sh_up"], p["w_sh_down"], layer=l)
    return _combine(y_slabs, pos, top_w, shared)


def kernel(x_prompt, x_sample, cache_conv, state_gdn, state_s5_re, state_s5_im, c_prompt, c_sample, ln_in_g, ln_in_b, w_ada, b_ada, w_in, conv_w, gdn_a_log, gdn_dt_bias, gdn_norm_w, s5_lam_re, s5_lam_im, s5_log_dt, s5_b_re, s5_b_im, s5_c_re, s5_c_im, s5_d, w_glu, w_br_a, w_br_b, w_out, ln1_g, ln1_b, w_router, router_bias, w_exp_gate, w_exp_up, w_exp_down, w_sh_gate, w_sh_up, w_sh_down, ln2_g, ln2_b):
    c_all = jnp.concatenate([c_prompt, jnp.zeros((SUBLANES - BATCH, D_MODEL), F32), c_sample], axis=0)
    mod = _ada_all(c_all, w_ada, b_ada)
    mod_p = mod[:, :SUBLANES]
    mod_s = mod[:, SUBLANES:]

    lane_pad = lambda a: jnp.pad(a, ((0, 0), (V_HEADS, LANES - 2 * V_HEADS)))[:, None, :]
    p = {
        "w_in": w_in,
        "w_ba": jnp.pad(w_in[:, :, COL_BA:COL_TAIL], ((0, 0), (0, 0), (0, LANES - 2 * V_HEADS))),
        "w_tail": w_in[:, :, COL_TAIL:],
        "alog_row": lane_pad(gdn_a_log), "dtb_row": lane_pad(gdn_dt_bias),
        "conv_w": conv_w, "gdn_norm_w": gdn_norm_w,
        "s5": jax.vmap(_s5_params)(s5_lam_re, s5_lam_im, s5_log_dt, s5_b_re, s5_b_im, s5_c_re, s5_c_im, s5_d),
        "w_glu": w_glu, "w_br_a": w_br_a, "w_br_b": w_br_b, "w_out": w_out,
        "w_router": w_router, "router_bias": router_bias,
        "w_exp_gate": w_exp_gate, "w_exp_up": w_exp_up, "w_exp_down": w_exp_down,
        "w_sh_gate": w_sh_gate, "w_sh_up": w_sh_up, "w_sh_down": w_sh_down,
    }
    cache_t = cache_conv.transpose(0, 2, 1, 3)
    s5_x0r = state_s5_re.reshape(DEPTH, DEC_BATCH, S5_STATES)
    s5_x0i = state_s5_im.reshape(DEPTH, DEC_BATCH, S5_STATES)

    x_all = jnp.concatenate([x_prompt.reshape(N_PROMPT, D_MODEL), x_sample.reshape(DEC_BATCH, D_MODEL)], axis=0)
    x, h1 = _ln_in(x_all, ln_in_g, ln_in_b, mod_p, mod_s)

    per_layer = []
    gdn_acc = jnp.zeros(state_gdn.shape, F32)
    for l in range(DEPTH):
        mix, states = _mixer(l, h1, p, cache_t, state_gdn, gdn_acc, s5_x0r, s5_x0i)
        gdn_acc = states[5]
        per_layer.append(states)
        x, h2 = _deepnorm(x, mix, ln1_g, ln1_b, mod_p, mod_s, layer=l, gate_comp=2, mod_layer=l,
                          sc_comp=4, sh_comp=3, h_dtype=F32)
        ffn = _moe(l, h2, p)
        nxt = min(l + 1, DEPTH - 1)
        x, h1 = _deepnorm(x, ffn, ln2_g, ln2_b, mod_p, mod_s, layer=l, gate_comp=5, mod_layer=nxt,
                          sc_comp=1, sh_comp=0, h_dtype=BF16)

    stacked = [gdn_acc if i == 5 else jnp.stack([s[i] for s in per_layer]) for i in range(8)]
    return (x[:N_PROMPT].reshape(BATCH, SEQ, D_MODEL), x[N_PROMPT:].reshape(DEC_BATCH, 1, D_MODEL), *stacked)
```
